```python
import math
import jax
import jax.numpy as jnp
from jax import lax
import numpy as np

D_MODEL = 2048
BATCH = 4
SEQ = 4096
DEPTH = 2

GRID_W = 64
CTX_LEN = 256
HEAD_DIM = 128
ROPE_BASE = 10000.0
NORM_EPS = 1e-6
A_HEADS = 6
A_KV_HEADS = 2
WINDOW = 128
WBLK = 128
B_HEADS = 6
B_DK = 128
B_DV = 128
CONV_K = 5
GDN_CHUNK = 64
C_HEADS = 6
C_Q_RANK = 512
C_KV_RANK = 256
C_NOPE = 128
C_ROPE = 64
C_V = 128
QBLK = 128
N_BRANCH = 3
BRANCH_W = (A_HEADS * HEAD_DIM, B_HEADS * B_DV, C_HEADS * C_V)
MIX_W = sum(BRANCH_W)
N_EXPERTS = 16
EC_CAPACITY = 2
EXPERT_FF = 2048
B_QKV_W = B_HEADS * (2 * B_DK + B_DV)
IN_SIZES = (
    A_HEADS * HEAD_DIM, A_KV_HEADS * HEAD_DIM, A_KV_HEADS * HEAD_DIM,
    B_QKV_W, B_HEADS * B_DV, 2 * B_HEADS, 2 * B_HEADS,
    C_Q_RANK, C_KV_RANK, C_ROPE,
    N_BRANCH * D_MODEL,
)
N_IN = sum(IN_SIZES)

kernel_name = 'hybrid_diffusion_trunk_swa_gdn_mla_ecmoe'


def split_cols(t, sizes):
    return jnp.split(t, np.cumsum(sizes)[:-1].tolist(), axis=-1)


def rms_norm(t, gain):
    tf = t.astype(jnp.float32)
    y = tf * lax.rsqrt(jnp.mean(tf * tf, axis=-1, keepdims=True) + NORM_EPS)
    return (y * gain.astype(jnp.float32)).astype(t.dtype)


def l2_normalize(t):
    tf = t.astype(jnp.float32)
    return tf * lax.rsqrt(jnp.sum(tf * tf, axis=-1, keepdims=True) + NORM_EPS)


def rope_2d(t, row, col):
    r = t.shape[-1]
    half = r // 2
    quarter = half // 2
    inv = ROPE_BASE ** (-jnp.arange(quarter, dtype=jnp.float32) / quarter)
    tf = t.astype(jnp.float32)

    def rot(u, pos):
        ang = pos.astype(jnp.float32)[:, None] * inv
        cos = jnp.cos(ang)[None, :, None, :]
        sin = jnp.sin(ang)[None, :, None, :]
        u1, u2 = u[..., :quarter], u[..., quarter:]
        return jnp.concatenate([u1 * cos - u2 * sin, u2 * cos + u1 * sin], axis=-1)

    out = jnp.concatenate([rot(tf[..., :half], row), rot(tf[..., half:], col)], axis=-1)
    return out.astype(t.dtype)


def window_attention(q, k, v, k_ctx, v_ctx, sink):
    B, S, HQ, D = q.shape
    HK = k.shape[2]
    G = HQ // HK
    nb = S // WBLK
    L = k_ctx.shape[1]
    scale = D ** -0.5
    qb = q.reshape(B, nb, WBLK, HK, G, D)

    def band(t):
        tp = jnp.pad(t, ((0, 0), (WBLK, WBLK), (0, 0), (0, 0))).reshape(B, nb + 2, WBLK, HK, D)
        return jnp.concatenate([tp[:, :-2], tp[:, 1:-1], tp[:, 2:]], axis=2)

    kb, vb = band(k), band(v)
    s_loc = jnp.einsum('bnqhgd,bnjhd->bnhgqj', qb, kb).astype(jnp.float32) * scale
    blk = jnp.arange(nb)[:, None, None] * WBLK
    qpos = blk + jnp.arange(WBLK)[None, :, None]
    kpos = blk + jnp.arange(3 * WBLK)[None, None, :] - WBLK
    valid = (jnp.abs(kpos - qpos) <= WINDOW) & (kpos >= 0) & (kpos < S)
    s_loc = jnp.where(valid[None, :, None, None], s_loc, -jnp.inf)
    s_ctx = jnp.einsum('bnqhgd,bchd->bnhgqc', qb, k_ctx).astype(jnp.float32) * scale
    s_sink = jnp.broadcast_to(sink.astype(jnp.float32).reshape(1, 1, HK, G, 1, 1), s_loc.shape[:-1] + (1,))
    p = jax.nn.softmax(jnp.concatenate([s_loc, s_ctx, s_sink], axis=-1), axis=-1).astype(v.dtype)
    o = (jnp.einsum('bnhgqj,bnjhd->bnqhgd', p[..., :3 * WBLK], vb)
         + jnp.einsum('bnhgqc,bchd->bnqhgd', p[..., 3 * WBLK:3 * WBLK + L], v_ctx))
    return o.reshape(B, S, HQ * D)


def context_attention(q, k, v, sink=None):
    B, T, HQ, DQK = q.shape
    HK = k.shape[2]
    G = HQ // HK
    DV = v.shape[-1]
    qg = q.reshape(B, T, HK, G, DQK)
    s = jnp.einsum('bqhgd,bkhd->bhgqk', qg, k).astype(jnp.float32) * DQK ** -0.5
    if sink is not None:
        s_sink = jnp.broadcast_to(sink.astype(jnp.float32).reshape(1, HK, G, 1, 1), s.shape[:-1] + (1,))
        s = jnp.concatenate([s, s_sink], axis=-1)
    p = jax.nn.softmax(s, axis=-1)[..., :T].astype(v.dtype)
    o = jnp.einsum('bhgqk,bkhd->bqhgd', p, v)
    return o.reshape(B, T, HQ * DV)


def mla_project(cq, ckv, ckr, c_q_gain, c_q_up, c_kv_gain, c_kv_up, pos):
    B, T, _ = cq.shape
    q = (rms_norm(cq, c_q_gain) @ c_q_up).reshape(B, T, C_HEADS, C_NOPE + C_ROPE)
    kv = (rms_norm(ckv, c_kv_gain) @ c_kv_up).reshape(B, T, C_HEADS, C_NOPE + C_V)
    q_nope, q_rope = q[..., :C_NOPE], q[..., C_NOPE:]
    k_nope, v = kv[..., :C_NOPE], kv[..., C_NOPE:]
    k_rope = ckr[:, :, None, :]
    if pos is not None:
        q_rope = rope_2d(q_rope, pos[0], pos[1])
        k_rope = rope_2d(k_rope, pos[0], pos[1])
    q = jnp.concatenate([q_nope, q_rope], axis=-1)
    k = jnp.concatenate([k_nope, jnp.broadcast_to(k_rope, (B, T, C_HEADS, C_ROPE))], axis=-1)
    return q, k, v


def mla_latent_attention(q, k, v, k_ctx, v_ctx):
    B, S, H, DQK = q.shape
    DV = v.shape[-1]
    nb = S // QBLK
    scale = DQK ** -0.5
    k_all = jnp.concatenate([k, k_ctx], axis=1)
    v_all = jnp.concatenate([v, v_ctx], axis=1)
    q_blocks = jnp.moveaxis(q.reshape(B, nb, QBLK, H, DQK), 1, 0)

    def attend(qb):
        s = jnp.einsum('bqhd,bkhd->bhqk', qb, k_all).astype(jnp.float32) * scale
        p = jax.nn.softmax(s, axis=-1).astype(v_all.dtype)
        return jnp.einsum('bhqk,bkhd->bqhd', p, v_all)

    o = lax.map(attend, q_blocks)
    return jnp.moveaxis(o, 0, 1).reshape(B, S, H * DV)


def short_conv(u, w):
    C = u.shape[-1]
    return lax.conv_general_dilated(
        u, w[:, None, :].astype(u.dtype), window_strides=(1,),
        padding=((CONV_K // 2, CONV_K // 2),),
        dimension_numbers=('NWC', 'WIO', 'NWC'), feature_group_count=C)


def gdn_prepare(qkv, a, b, w_conv, a_log, dt_bias):
    B, T, _ = qkv.shape
    u = jax.nn.silu(short_conv(qkv, w_conv))
    q, k, v = split_cols(u, (B_HEADS * B_DK, B_HEADS * B_DK, B_HEADS * B_DV))
    q = l2_normalize(q.reshape(B, T, B_HEADS, B_DK)) * B_DK ** -0.5
    k = l2_normalize(k.reshape(B, T, B_HEADS, B_DK))
    v = v.reshape(B, T, B_HEADS, B_DV).astype(jnp.float32)
    beta = jax.nn.sigmoid(b.astype(jnp.float32)).reshape(B, T, 2, B_HEADS)
    g = -jnp.exp(a_log.astype(jnp.float32)) * jax.nn.softplus(
        a.astype(jnp.float32).reshape(B, T, 2, B_HEADS) + dt_bias.astype(jnp.float32))
    return q, k, v, beta, g


def gated_delta_chunked(q, k, v, beta, g, state):
    B, T, H, DK = q.shape
    DV = v.shape[-1]
    C = GDN_CHUNK
    n = T // C

    def chunks(t):
        t = t.astype(jnp.float32).reshape((B, n, C, H) + t.shape[3:])
        return jnp.moveaxis(jnp.moveaxis(t, 1, 0), 3, 2)

    qc, kc, vc, bc, gc = chunks(q), chunks(k), chunks(v), chunks(beta), chunks(g)
    decay = jnp.cumsum(gc, axis=-1)
    tri = jnp.tril(jnp.ones((C, C), dtype=bool))
    gamma = jnp.exp(jnp.where(tri, decay[..., :, None] - decay[..., None, :], -jnp.inf))
    kb = kc * bc[..., None]
    a_strict = jnp.tril(jnp.einsum('nbhid,nbhjd->nbhij', kb, kc) * gamma, -1)
    m = a_strict + jnp.eye(C, dtype=jnp.float32)
    rhs = jnp.concatenate([vc * bc[..., None], kb * jnp.exp(decay)[..., None]], axis=-1)
    sol = lax.linalg.triangular_solve(m, rhs, left_side=True, lower=True)
    u, w = sol[..., :DV], sol[..., DV:]
    a_qk = jnp.einsum('nbhid,nbhjd->nbhij', qc, kc) * gamma

    def step(s, xs):
        q_i, k_i, u_i, w_i, d_i, a_i = xs
        v_new = u_i - w_i @ s
        o = (q_i * jnp.exp(d_i)[..., None]) @ s + a_i @ v_new
        d_last = d_i[..., -1:]
        s = s * jnp.exp(d_last)[..., None] + jnp.einsum(
            'bhcd,bhce->bhde', k_i * jnp.exp(d_last - d_i)[..., None], v_new)
        return s, o

    s_fin, o = lax.scan(step, state.astype(jnp.float32), (qc, kc, u, w, decay, a_qk))
    o = jnp.moveaxis(jnp.moveaxis(o, 0, 1), 2, 3).reshape(B, T, H, DV)
    return o, s_fin


def gdn_bidirectional(q, k, v, beta, g, init_f, init_b):
    rev = lambda t: t[:, ::-1]
    o_f, s_f = gated_delta_chunked(q, k, v, beta[:, :, 0], g[:, :, 0], init_f)
    o_b, s_b = gated_delta_chunked(rev(q), rev(k), rev(v), rev(beta[:, :, 1]), rev(g[:, :, 1]), init_b)
    return o_f + rev(o_b), s_f, s_b


def gdn_output(o, z, gain):
    B, T, H, DV = o.shape
    y = rms_norm(o.astype(z.dtype), gain) * jax.nn.silu(z.reshape(B, T, H, DV))
    return y.reshape(B, T, H * DV)


def merge_branches(ya, yb, yc, gate_logits, w_branch, w_o):
    wa, wb, wc = jnp.split(w_branch, np.cumsum(BRANCH_W)[:-1].tolist(), axis=0)
    ga, gb, gc = jnp.split(gate_logits, N_BRANCH, axis=-1)
    mixed = (jax.nn.sigmoid(ga) * (ya @ wa) + jax.nn.sigmoid(gb) * (yb @ wb)
             + jax.nn.sigmoid(gc) * (yc @ wc))
    return mixed @ w_o


def token_mixer(h, hc, row, col, w_in, a_sink, b_conv, b_a_log, b_dt_bias, b_out_gain,
                c_q_gain, c_q_up, c_kv_gain, c_kv_up, w_branch, w_o, need_ctx):
    B, S, _ = h.shape
    L = hc.shape[1]
    (aq, ak, av, bqkv, bz, ba, bb, cq, ckv, ckr, gt) = split_cols(h @ w_in, IN_SIZES)
    (aq_c, ak_c, av_c, bqkv_c, bz_c, ba_c, bb_c, cq_c, ckv_c, ckr_c, gt_c) = split_cols(hc @ w_in, IN_SIZES)

    qa = rope_2d(aq.reshape(B, S, A_HEADS, HEAD_DIM), row, col)
    ka = rope_2d(ak.reshape(B, S, A_KV_HEADS, HEAD_DIM), row, col)
    va = av.reshape(B, S, A_KV_HEADS, HEAD_DIM)
    ka_c = ak_c.reshape(B, L, A_KV_HEADS, HEAD_DIM)
    va_c = av_c.reshape(B, L, A_KV_HEADS, HEAD_DIM)
    ya = window_attention(qa, ka, va, ka_c, va_c, a_sink)

    zero_state = jnp.zeros((B, B_HEADS, B_DK, B_DV), jnp.float32)
    q_bc, k_bc, v_bc, beta_c, g_c = gdn_prepare(bqkv_c, ba_c, bb_c, b_conv, b_a_log, b_dt_bias)
    ob_c, s_f, s_b = gdn_bidirectional(q_bc, k_bc, v_bc, beta_c, g_c, zero_state, zero_state)
    q_bl, k_bl, v_bl, beta_l, g_l = gdn_prepare(bqkv, ba, bb, b_conv, b_a_log, b_dt_bias)
    ob, _, _ = gdn_bidirectional(q_bl, k_bl, v_bl, beta_l, g_l, s_f, s_b)
    yb = gdn_output(ob, bz, b_out_gain)

    qc_l, kc_l, vc_l = mla_project(cq, ckv, ckr, c_q_gain, c_q_up, c_kv_gain, c_kv_up, (row, col))
    qc_c, kc_c, vc_c = mla_project(cq_c, ckv_c, ckr_c, c_q_gain, c_q_up, c_kv_gain, c_kv_up, None)
    yc = mla_latent_attention(qc_l, kc_l, vc_l, kc_c, vc_c)

    y = merge_branches(ya, yb, yc, gt, w_branch, w_o)
    if not need_ctx:
        return y, None
    ya_c = context_attention(aq_c.reshape(B, L, A_HEADS, HEAD_DIM), ka_c, va_c, a_sink)
    yb_c = gdn_output(ob_c, bz_c, b_out_gain)
    yc_c = context_attention(qc_c, kc_c, vc_c)
    return y, merge_branches(ya_c, yb_c, yc_c, gt_c, w_branch, w_o)


def expert_choice_ffn(h, w_router, b_router, w_gate, w_up, w_down):
    B, T, D = h.shape
    cap = EC_CAPACITY * T // N_EXPERTS
    logits = (h @ w_router).astype(jnp.float32) + b_router.astype(jnp.float32)
    aff = jax.nn.softmax(logits, axis=-1)
    g, idx = lax.top_k(jnp.swapaxes(aff, 1, 2), cap)
    xs = jax.vmap(lambda hb, ib: hb[ib])(h, idx)
    hid = jax.nn.silu(jnp.einsum('becd,edf->becf', xs, w_gate)) * jnp.einsum('becd,edf->becf', xs, w_up)
    y = jnp.einsum('becf,efd->becd', hid, w_down) * g[..., None].astype(xs.dtype)
    return jax.vmap(lambda ib, yb: jnp.zeros((T, D), yb.dtype).at[ib.reshape(-1)].add(yb.reshape(-1, D)))(idx, y)


def setup_inputs(seed: int = 0) -> dict:
    key = jax.random.key(seed)
    ks = jax.random.split(key, 30)
    f32 = jnp.float32
    nrm = lambda k, shape, s: jax.random.normal(k, shape, f32) * s
    L = DEPTH
    dt = jnp.exp(jax.random.uniform(ks[12], (L, 2, B_HEADS), f32, math.log(1e-3), math.log(1e-1)))
    return {
        'x': nrm(ks[0], (BATCH, SEQ, D_MODEL), 1.0),
        'c': nrm(ks[1], (BATCH, D_MODEL), 1.0),
        'ctx': nrm(ks[2], (BATCH, CTX_LEN, D_MODEL), 1.0),
        'c_ctx': nrm(ks[3], (D_MODEL,), 1.0),
        'w_mod': nrm(ks[4], (L, D_MODEL, 6 * D_MODEL), 0.5 * D_MODEL ** -0.5),
        'b_mod': nrm(ks[5], (L, 6 * D_MODEL), 0.02),
        'g_norm1': 1.0 + nrm(ks[6], (L, D_MODEL), 0.02),
        'g_norm2': 1.0 + nrm(ks[7], (L, D_MODEL), 0.02),
        'w_in': nrm(ks[8], (L, D_MODEL, N_IN), D_MODEL ** -0.5),
        'a_sink': nrm(ks[9], (L, A_HEADS), 0.5),
        'b_conv': nrm(ks[10], (L, CONV_K, B_QKV_W), CONV_K ** -0.5),
        'b_a_log': jnp.log(jax.random.uniform(ks[11], (L, 2, B_HEADS), f32, 0.5, 4.0)),
        'b_dt_bias': dt + jnp.log(-jnp.expm1(-dt)),
        'b_out_gain': 1.0 + nrm(ks[13], (L, B_DV), 0.02),
        'c_q_gain': 1.0 + nrm(ks[14], (L, C_Q_RANK), 0.02),
        'c_q_up': nrm(ks[15], (L, C_Q_RANK, C_HEADS * (C_NOPE + C_ROPE)), C_Q_RANK ** -0.5),
        'c_kv_gain': 1.0 + nrm(ks[16], (L, C_KV_RANK), 0.02),
        'c_kv_up': nrm(ks[17], (L, C_KV_RANK, C_HEADS * (C_NOPE + C_V)), C_KV_RANK ** -0.5),
        'w_branch': nrm(ks[18], (L, MIX_W, D_MODEL), (MIX_W // N_BRANCH) ** -0.5),
        'w_o': nrm(ks[19], (L, D_MODEL, D_MODEL), D_MODEL ** -0.5),
        'w_router': nrm(ks[20], (L, D_MODEL, N_EXPERTS), D_MODEL ** -0.5),
        'b_router': nrm(ks[21], (L, N_EXPERTS), 0.01),
        'w_exp_gate': nrm(ks[22], (L, N_EXPERTS, D_MODEL, EXPERT_FF), D_MODEL ** -0.5),
        'w_exp_up': nrm(ks[23], (L, N_EXPERTS, D_MODEL, EXPERT_FF), D_MODEL ** -0.5),
        'w_exp_down': nrm(ks[24], (L, N_EXPERTS, EXPERT_FF, D_MODEL), EXPERT_FF ** -0.5),
        'g_final': 1.0 + nrm(ks[25], (D_MODEL,), 0.02),
    }


def reference(x, c, ctx, c_ctx, w_mod, b_mod, g_norm1, g_norm2, w_in, a_sink, b_conv, b_a_log,
              b_dt_bias, b_out_gain, c_q_gain, c_q_up, c_kv_gain, c_kv_up, w_branch, w_o,
              w_router, b_router, w_exp_gate, w_exp_up, w_exp_down, g_final):
    B, S, D = x.shape
    rows = S // GRID_W
    row = jnp.repeat(jnp.arange(rows, dtype=jnp.int32), GRID_W)
    col = jnp.tile(jnp.arange(GRID_W, dtype=jnp.int32), rows)
    silu_c = jax.nn.silu(c)
    silu_cc = jax.nn.silu(c_ctx)[None]
    xc = ctx
    for l in range(DEPTH):
        need_ctx = l < DEPTH - 1
        mod = (silu_c @ w_mod[l] + b_mod[l])[:, None, :]
        mod_c = (silu_cc @ w_mod[l] + b_mod[l])[:, None, :]
        sh1, sc1, gt1, sh2, sc2, gt2 = jnp.split(mod, 6, axis=-1)
        sh1c, sc1c, gt1c, sh2c, sc2c, gt2c = jnp.split(mod_c, 6, axis=-1)
        h = rms_norm(x, g_norm1[l]) * (1.0 + sc1) + sh1
        hc = rms_norm(xc, g_norm1[l]) * (1.0 + sc1c) + sh1c
        y, y_c = token_mixer(h, hc, row, col, w_in[l], a_sink[l], b_conv[l], b_a_log[l], b_dt_bias[l],
                             b_out_gain[l], c_q_gain[l], c_q_up[l], c_kv_gain[l], c_kv_up[l],
                             w_branch[l], w_o[l], need_ctx)
        x = x + gt1 * y
        h2 = rms_norm(x, g_norm2[l]) * (1.0 + sc2) + sh2
        x = x + gt2 * expert_choice_ffn(h2, w_router[l], b_router[l], w_exp_gate[l], w_exp_up[l], w_exp_down[l])
        if need_ctx:
            xc = xc + gt1c * y_c
            h2c = rms_norm(xc, g_norm2[l]) * (1.0 + sc2c) + sh2c
            xc = xc + gt2c * expert_choice_ffn(h2c, w_router[l], b_router[l], w_exp_gate[l], w_exp_up[l], w_exp_down[l])
    return rms_norm(x, g_final)
```

```python
import functools
import math

import numpy as np
import jax
import jax.numpy as jnp
from jax import lax
from jax.experimental import pallas as pl
from jax.experimental.pallas import tpu as pltpu

f32 = jnp.float32
bf16 = jnp.bfloat16
i32 = jnp.int32

GRID_W = 64
HEAD_DIM = 128
ROPE_BASE = 10000.0
NORM_EPS = 1e-6
A_HEADS = 6
A_KV_HEADS = 2
A_GROUP = A_HEADS // A_KV_HEADS
WINDOW = 128
WBLK = 128
B_HEADS = 6
B_DK = 128
B_DV = 128
CONV_K = 5
GDN_CHUNK = 64
C_HEADS = 6
C_Q_RANK = 512
C_KV_RANK = 256
C_NOPE = 128
C_ROPE = 64
C_V = 128
N_EXPERTS = 16
EC_CAPACITY = 2

LANE = 128
NEG = -1e30
VMEM_LIMIT = 56 * 1024 * 1024

COL_GT = 0
REL_CQ = 0
REL_CKV = 512
REL_CKR = 768
REL_AB = 896
REL_AQ = 1024
REL_AK = 1792
REL_AV = 2048
REL_BQKV = 2304
REL_BZ = 4608
REL_END = 5376


def _cparams(sem):
    return pltpu.CompilerParams(dimension_semantics=sem, vmem_limit_bytes=VMEM_LIMIT)


def _pick(cands, *ns):
    for c in cands:
        if all(n % c == 0 for n in ns):
            return c
    raise ValueError(f"no tile in {cands} divides {ns}")


def _dot(a, b):
    return jnp.dot(a, b, preferred_element_type=f32)


def _dot_nt(a, b):
    return lax.dot_general(a, b, (((1,), (1,)), ((), ())), preferred_element_type=f32)


def _dot_tn(a, b):
    return lax.dot_general(a, b, (((0,), (0,)), ((), ())), preferred_element_type=f32)


def _split3(x):
    hi = x.astype(bf16)
    r = x - hi.astype(f32)
    mid = r.astype(bf16)
    lo = (r - mid.astype(f32)).astype(bf16)
    return hi, mid, lo


def _mod_kernel(c_ref, w_ref, b_ref, o_ref):
    c = c_ref[...]
    sc = (c * jax.nn.sigmoid(c)).astype(bf16)
    o_ref[0] = _dot(sc, w_ref[0].astype(bf16)) + b_ref[0]


def _mod(c_all, w_mod, b_mod):
    n_layers, d, n = w_mod.shape
    tn = _pick((1024, 512, 256, 128), n)
    return pl.pallas_call(
        _mod_kernel,
        grid=(n_layers, n // tn),
        in_specs=[
            pl.BlockSpec((8, d), lambda l, j: (0, 0)),
            pl.BlockSpec((1, d, tn), lambda l, j: (l, 0, j)),
            pl.BlockSpec((1, 1, tn), lambda l, j: (l, 0, j)),
        ],
        out_specs=pl.BlockSpec((1, 8, tn), lambda l, j: (l, 0, j)),
        out_shape=jax.ShapeDtypeStruct((n_layers, 8, n), f32),
        compiler_params=_cparams(("arbitrary", "arbitrary")),
        name="adaln_mod",
    )(c_all, w_mod, b_mod.reshape(n_layers, 1, n))


def _norm_matmul_kernel(*refs, has_res, side, eps):
    it = iter(refs)
    x_ref = next(it)
    d_ref = g_ref = None
    if has_res:
        d_ref, g_ref = next(it), next(it)
    s_ref, sh_ref, w_ref, o_ref = next(it), next(it), next(it), next(it)
    xn_ref = next(it) if has_res else None
    side_ref = next(it) if side is not None else None
    h_scr = next(it)
    j = pl.program_id(1)

    @pl.when(j == 0)
    def _():
        x = x_ref[...].astype(f32)
        if has_res:
            x = x + g_ref[0] * d_ref[...].astype(f32)
            xn_ref[...] = x
        ms = jnp.mean(x * x, axis=-1, keepdims=True)
        h = x * lax.rsqrt(ms + eps) * s_ref[0] + sh_ref[0]
        h_scr[...] = h.astype(bf16)

    acc = _dot(h_scr[...], w_ref[...])
    o_ref[...] = acc.astype(o_ref.dtype)
    if side is not None:
        side_j, side_off = side

        @pl.when(j == side_j)
        def _():
            side_ref[...] = acc[:, side_off:side_off + LANE]


def _norm_matmul(x, x_col, k, scale, shift, w, group_of_tile, tm, tn, *, rows=None, delta=None, gate=None,
                 side_col=None, name):
    rows = x.shape[0] if rows is None else rows
    n = w.shape[1]
    assert rows % tm == 0 and n % tn == 0 and w.shape[0] == k
    has_res = delta is not None
    side = None
    if side_col is not None:
        side = (side_col // tn, side_col % tn)
        assert side[1] + LANE <= tn
    in_specs = [pl.BlockSpec((tm, k), lambda i, j: (i, x_col))]
    args = [x]
    if has_res:
        in_specs += [pl.BlockSpec((tm, k), lambda i, j: (i, 0)),
                     pl.BlockSpec((1, 1, k), lambda i, j: (group_of_tile(i), 0, 0))]
        args += [delta, gate]
    in_specs += [pl.BlockSpec((1, 1, k), lambda i, j: (group_of_tile(i), 0, 0)),
                 pl.BlockSpec((1, 1, k), lambda i, j: (group_of_tile(i), 0, 0)),
                 pl.BlockSpec((k, tn), lambda i, j: (0, j))]
    args += [scale, shift, w]
    out_specs = [pl.BlockSpec((tm, tn), lambda i, j: (i, j))]
    out_shape = [jax.ShapeDtypeStruct((rows, n), bf16)]
    if has_res:
        out_specs.append(pl.BlockSpec((tm, k), lambda i, j: (i, 0)))
        out_shape.append(jax.ShapeDtypeStruct((rows, k), f32))
    if side is not None:
        out_specs.append(pl.BlockSpec((tm, LANE), lambda i, j: (i, 0)))
        out_shape.append(jax.ShapeDtypeStruct((rows, LANE), f32))
    return pl.pallas_call(
        functools.partial(_norm_matmul_kernel, has_res=has_res, side=side, eps=NORM_EPS),
        grid=(rows // tm, n // tn),
        in_specs=in_specs,
        out_specs=out_specs,
        out_shape=out_shape,
        scratch_shapes=[pltpu.VMEM((tm, k), bf16)],
        compiler_params=_cparams(("arbitrary", "arbitrary")),
        name=name,
    )(*args)


def _final_norm_kernel(x_ref, d_ref, g_ref, gain_ref, o_ref, *, eps):
    x = x_ref[...] + g_ref[0] * d_ref[...]
    ms = jnp.mean(x * x, axis=-1, keepdims=True)
    o_ref[...] = x * lax.rsqrt(ms + eps) * gain_ref[...]


def _final_norm(x, delta, gate, gain, group_of_tile, tm, rows):
    d = x.shape[1]
    return pl.pallas_call(
        functools.partial(_final_norm_kernel, eps=NORM_EPS),
        grid=(rows // tm,),
        in_specs=[pl.BlockSpec((tm, d), lambda i: (i, 0)),
                  pl.BlockSpec((tm, d), lambda i: (i, 0)),
                  pl.BlockSpec((1, 1, d), lambda i: (group_of_tile(i), 0, 0)),
                  pl.BlockSpec((1, d), lambda i: (0, 0))],
        out_specs=pl.BlockSpec((tm, d), lambda i: (i, 0)),
        out_shape=jax.ShapeDtypeStruct((rows, d), f32),
        compiler_params=_cparams(("arbitrary",)),
        name="final_norm",
    )(x, delta, gate, gain.reshape(1, d))


def _rope_tables(batch, seq, n_ctx_rows, rot):
    half, quarter = rot // 2, rot // 4
    t = np.arange(seq)
    row, col = (t // GRID_W).astype(np.float32), (t % GRID_W).astype(np.float32)
    inv = (ROPE_BASE ** (-np.arange(quarter, dtype=np.float32) / quarter)).astype(np.float32)
    lane = np.arange(LANE)
    pos = np.where((lane % rot) < half, row[:, None], col[:, None]).astype(np.float32)
    ang = pos * inv[lane % quarter][None, :]
    active = (lane < rot)[None, :]
    cos = np.where(active, np.cos(ang), 1.0)
    sin = np.where(active, np.sin(ang), 0.0)
    first = ((lane % half) < quarter)[None, :]
    s1 = np.where(first, -sin, 0.0)
    s2 = np.where(first, 0.0, sin)

    def full(tab, fill):
        lat = np.tile(tab.astype(np.float32), (batch, 1))
        ctx = np.full((n_ctx_rows, LANE), fill, np.float32)
        return jnp.asarray(np.concatenate([lat, ctx], axis=0))

    return full(cos, 1.0), full(s1, 0.0), full(s2, 0.0)


def _rope(x, cos, s1, s2, quarter):
    return x * cos + pltpu.roll(x, LANE - quarter, 1) * s1 + pltpu.roll(x, quarter, 1) * s2


def _prep_a_kernel(p_ref, cos_ref, s1_ref, s2_ref, o_ref, *, scale):
    cos, s1, s2 = cos_ref[...], s1_ref[...], s2_ref[...]
    for h in range(A_HEADS + A_KV_HEADS):
        sl = slice(h * LANE, (h + 1) * LANE)
        y = _rope(p_ref[:, sl].astype(f32), cos, s1, s2, HEAD_DIM // 4)
        if h < A_HEADS:
            y = y * scale
        o_ref[:, sl] = y.astype(bf16)


def _prep_a(proj, tabs, col_blk, tm):
    rows = proj.shape[0]
    w = (A_HEADS + A_KV_HEADS) * LANE
    tab_spec = pl.BlockSpec((tm, LANE), lambda i: (i, 0))
    return pl.pallas_call(
        functools.partial(_prep_a_kernel, scale=HEAD_DIM ** -0.5),
        grid=(rows // tm,),
        in_specs=[pl.BlockSpec((tm, w), lambda i: (i, col_blk)), tab_spec, tab_spec, tab_spec],
        out_specs=pl.BlockSpec((tm, w), lambda i: (i, 0)),
        out_shape=jax.ShapeDtypeStruct((rows, w), bf16),
        compiler_params=_cparams(("arbitrary",)),
        name="rope_a",
    )(proj, *tabs)


def _prep_c_kernel(q_ref, kn_ref, kr_ref, cos_ref, s1_ref, s2_ref, oq_ref, ok_ref, *, scale):
    cos, s1, s2 = cos_ref[...], s1_ref[...], s2_ref[...]
    quarter = C_ROPE // 4
    kr = _rope(kr_ref[...].astype(f32), cos, s1, s2, quarter).astype(bf16)
    for h in range(C_HEADS):
        a, b, c = 2 * h * LANE, (2 * h + 1) * LANE, (2 * h + 2) * LANE
        oq_ref[:, a:b] = (q_ref[:, a:b].astype(f32) * scale).astype(bf16)
        oq_ref[:, b:c] = (_rope(q_ref[:, b:c].astype(f32), cos, s1, s2, quarter) * scale).astype(bf16)
        ok_ref[:, a:b] = kn_ref[:, h * LANE:(h + 1) * LANE]
        ok_ref[:, b:c] = kr


def _prep_c(qp, kvp, proj, tabs, ckr_blk, tm):
    rows = qp.shape[0]
    w = C_HEADS * 2 * LANE
    tab_spec = pl.BlockSpec((tm, LANE), lambda i: (i, 0))
    return pl.pallas_call(
        functools.partial(_prep_c_kernel, scale=(C_NOPE + C_ROPE) ** -0.5),
        grid=(rows // tm,),
        in_specs=[pl.BlockSpec((tm, w), lambda i: (i, 0)),
                  pl.BlockSpec((tm, C_HEADS * LANE), lambda i: (i, 0)),
                  pl.BlockSpec((tm, LANE), lambda i: (i, ckr_blk)),
                  tab_spec, tab_spec, tab_spec],
        out_specs=[pl.BlockSpec((tm, w), lambda i: (i, 0)), pl.BlockSpec((tm, w), lambda i: (i, 0))],
        out_shape=[jax.ShapeDtypeStruct((rows, w), bf16), jax.ShapeDtypeStruct((rows, w), bf16)],
        compiler_params=_cparams(("arbitrary",)),
        name="rope_c",
    )(qp, kvp, proj, *tabs)


def _softmax_attend(q, pieces, sink_col=None):
    scores = []
    for k, _, mask in pieces:
        s = _dot_nt(q, k)
        if mask is not None:
            s = jnp.where(mask, s, NEG)
        scores.append(s)
    m = None
    for s in scores:
        ms = jnp.max(s, axis=-1, keepdims=True)
        m = ms if m is None else jnp.maximum(m, ms)
    if sink_col is not None:
        m = jnp.maximum(m, sink_col)
    l = None
    o = None
    for s, (_, v, _) in zip(scores, pieces):
        p = jnp.exp(s - m)
        ls = jnp.sum(p, axis=-1, keepdims=True)
        os_ = _dot(p.astype(bf16), v)
        l = ls if l is None else l + ls
        o = os_ if o is None else o + os_
    if sink_col is not None:
        l = l + jnp.exp(sink_col - m)
    return o / l


def _win_attn_kernel(sink_ref, q_ref, kp_ref, kc_ref, kn_ref, vp_ref, vc_ref, vn_ref, kx_ref, vx_ref, o_ref, *,
                     nb, seq):
    n = pl.program_id(1)
    is_lat = n < nb
    g3 = A_GROUP * WBLK
    r = lax.broadcasted_iota(i32, (g3, 3 * WBLK), 0)
    c = lax.broadcasted_iota(i32, (g3, 3 * WBLK), 1)
    qpos = n * WBLK + (r % WBLK)
    kpos = (n - 1) * WBLK + c
    band = (jnp.abs(kpos - qpos) <= WINDOW) & (kpos >= 0) & (kpos < seq) & is_lat
    rcol = lax.broadcasted_iota(i32, (g3, 1), 0)
    for hk in range(A_KV_HEADS):
        ks = slice(hk * LANE, (hk + 1) * LANE)
        q = jnp.concatenate(
            [q_ref[:, (hk * A_GROUP + g) * LANE:(hk * A_GROUP + g + 1) * LANE] for g in range(A_GROUP)], axis=0)
        kb = jnp.concatenate([kp_ref[:, ks], kc_ref[:, ks], kn_ref[:, ks]], axis=0)
        vb = jnp.concatenate([vp_ref[:, ks], vc_ref[:, ks], vn_ref[:, ks]], axis=0)
        sink_col = jnp.zeros((g3, 1), f32)
        for g in range(A_GROUP):
            sink_col = jnp.where(rcol // WBLK == g, sink_ref[hk * A_GROUP + g], sink_col)
        o = _softmax_attend(q, [(kb, vb, band), (kx_ref[:, ks], vx_ref[:, ks], None)], sink_col)
        for g in range(A_GROUP):
            h = hk * A_GROUP + g
            o_ref[:, h * LANE:(h + 1) * LANE] = o[g * WBLK:(g + 1) * WBLK].astype(bf16)


def _win_attn(aqk, proj, av_blk, sink, batch, seq, ctx_len, with_ctx):
    nl = batch * seq
    nb = seq // WBLK
    ncb = ctx_len // WBLK
    steps = nb + (ncb if with_ctx else 0)
    rows = nl + (batch * ctx_len if with_ctx else 0)
    qw = A_HEADS * LANE
    kw = A_KV_HEADS * LANE
    k_blk = qw // kw

    def qrow(b, n):
        return jnp.where(n < nb, b * nb + n, nl // WBLK + b * ncb + (n - nb))

    def krow(d):
        return lambda b, n: (b * nb + jnp.clip(n + d, 0, nb - 1), k_blk)

    def vrow(d):
        return lambda b, n: (b * nb + jnp.clip(n + d, 0, nb - 1), av_blk)

    return pl.pallas_call(
        functools.partial(_win_attn_kernel, nb=nb, seq=seq),
        grid=(batch, steps),
        in_specs=[pl.BlockSpec(memory_space=pltpu.SMEM),
                  pl.BlockSpec((WBLK, qw), lambda b, n: (qrow(b, n), 0)),
                  pl.BlockSpec((WBLK, kw), krow(-1)), pl.BlockSpec((WBLK, kw), krow(0)),
                  pl.BlockSpec((WBLK, kw), krow(1)),
                  pl.BlockSpec((WBLK, kw), vrow(-1)), pl.BlockSpec((WBLK, kw), vrow(0)),
                  pl.BlockSpec((WBLK, kw), vrow(1)),
                  pl.BlockSpec((ctx_len, kw), lambda b, n: (nl // ctx_len + b, k_blk)),
                  pl.BlockSpec((ctx_len, kw), lambda b, n: (nl // ctx_len + b, av_blk))],
        out_specs=pl.BlockSpec((WBLK, qw), lambda b, n: (qrow(b, n), 0)),
        out_shape=jax.ShapeDtypeStruct((rows, qw), bf16),
        compiler_params=_cparams(("arbitrary", "arbitrary")),
        name="window_attention",
    )(sink, aqk, aqk, aqk, aqk, proj, proj, proj, aqk, proj)


def _mla_attn_kernel(q_ref, kl_ref, vl_ref, kx_ref, vx_ref, o_ref, *, nq):
    t = pl.program_id(2)
    lat_mask = jnp.broadcast_to(t < nq, (q_ref.shape[0], kl_ref.shape[0]))
    o = _softmax_attend(q_ref[...], [(kl_ref[...], vl_ref[...], lat_mask), (kx_ref[...], vx_ref[...], None)])
    o_ref[...] = o.astype(bf16)


def _mla_attn(cq, ck, kvp, batch, seq, ctx_len, tq, with_ctx):
    nl = batch * seq
    nq = seq // tq
    ncq = ctx_len // tq
    steps = nq + (ncq if with_ctx else 0)
    rows = nl + (batch * ctx_len if with_ctx else 0)

    def qrow(b, t):
        return jnp.where(t < nq, b * nq + t, nl // tq + b * ncq + (t - nq))

    return pl.pallas_call(
        functools.partial(_mla_attn_kernel, nq=nq),
        grid=(batch, C_HEADS, steps),
        in_specs=[pl.BlockSpec((tq, 2 * LANE), lambda b, h, t: (qrow(b, t), h)),
                  pl.BlockSpec((seq, 2 * LANE), lambda b, h, t: (b, h)),
                  pl.BlockSpec((seq, LANE), lambda b, h, t: (b, C_HEADS + h)),
                  pl.BlockSpec((ctx_len, 2 * LANE), lambda b, h, t: (nl // ctx_len + b, h)),
                  pl.BlockSpec((ctx_len, LANE), lambda b, h, t: (nl // ctx_len + b, C_HEADS + h))],
        out_specs=pl.BlockSpec((tq, LANE), lambda b, h, t: (qrow(b, t), h)),
        out_shape=jax.ShapeDtypeStruct((rows, C_HEADS * LANE), bf16),
        compiler_params=_cparams(("arbitrary", "arbitrary", "arbitrary")),
        name="mla_attention",
    )(cq, ck, kvp, ck, kvp)


HALO = 16


def _gdn_prep_kernel(x_ref, xp_ref, xn_ref, w_ref, pab_ref, alog_ref, dtb_ref, o_ref, gb_ref, scr, *,
                     tm, lat_tiles, seq_tiles, ctx_tiles):
    i = pl.program_id(0)
    p = pl.program_id(1)
    is_lat = i < lat_tiles
    pos = jnp.where(is_lat, i % seq_tiles, (i - lat_tiles) % ctx_tiles)
    last = jnp.where(is_lat, seq_tiles - 1, ctx_tiles - 1)
    has_prev = (pos != 0).astype(f32)
    has_next = (pos != last).astype(f32)
    scr[0:8, :] = xp_ref[...].astype(f32)[HALO - 8:HALO, :] * has_prev
    scr[8:8 + tm, :] = x_ref[...].astype(f32)
    scr[8 + tm:16 + tm, :] = xn_ref[...].astype(f32)[0:8, :] * has_next
    w = w_ref[...]
    acc = None
    for k in range(CONV_K):
        term = scr[pl.ds(8 - CONV_K // 2 + k, tm), :] * w[k:k + 1, :]
        acc = term if acc is None else acc + term
    u = acc * jax.nn.sigmoid(acc)
    scale = jnp.where(p == 0, B_DK ** -0.5, 1.0)
    for h in range(B_HEADS):
        sl = slice(h * LANE, (h + 1) * LANE)
        uh = u[:, sl]
        nrm = uh * lax.rsqrt(jnp.sum(uh * uh, axis=-1, keepdims=True) + NORM_EPS) * scale
        o_ref[:, sl] = jnp.where(p == 2, uh, nrm).astype(bf16)

    @pl.when(p == 0)
    def _():
        pab = pab_ref[...]
        lane = lax.broadcasted_iota(i32, pab.shape, 1)
        t = pab + dtb_ref[...]
        softplus = jnp.maximum(t, 0.0) + jnp.log(1.0 + jnp.exp(-jnp.abs(t)))
        g = -jnp.exp(alog_ref[...]) * softplus
        beta = jax.nn.sigmoid(pab)
        gb_ref[...] = jnp.where(lane < 2 * B_HEADS, g, jnp.where(lane < 4 * B_HEADS, beta, 0.0))


def _gdn_prep(proj, pab, w_conv, a_log, dt_bias, st):
    rows, tm = st["rows"], st["tm_seq"]
    d3 = 3 * st["d"]
    w = B_HEADS * LANE
    col0 = (d3 + REL_BQKV) // w
    assert (d3 + REL_BQKV) % w == 0 and tm % HALO == 0
    pad = lambda a: jnp.concatenate([a.reshape(1, -1).astype(f32), jnp.zeros((1, LANE - 2 * B_HEADS), f32)], axis=1)
    hb = tm // HALO
    n_halo = rows // HALO
    kern = functools.partial(_gdn_prep_kernel, tm=tm, lat_tiles=st["nl"] // tm, seq_tiles=st["seq"] // tm,
                             ctx_tiles=st["ctx_len"] // tm)
    return pl.pallas_call(
        kern,
        grid=(rows // tm, 3),
        in_specs=[pl.BlockSpec((tm, w), lambda i, p: (i, col0 + p)),
                  pl.BlockSpec((HALO, w), lambda i, p: (jnp.maximum(i * hb - 1, 0), col0 + p)),
                  pl.BlockSpec((HALO, w), lambda i, p: (jnp.minimum((i + 1) * hb, n_halo - 1), col0 + p)),
                  pl.BlockSpec((CONV_K, w), lambda i, p: (0, p)),
                  pl.BlockSpec((tm, LANE), lambda i, p: (i, 0)),
                  pl.BlockSpec((1, LANE), lambda i, p: (0, 0)),
                  pl.BlockSpec((1, LANE), lambda i, p: (0, 0))],
        out_specs=[pl.BlockSpec((tm, w), lambda i, p: (i, p)), pl.BlockSpec((tm, LANE), lambda i, p: (i, 0))],
        out_shape=[jax.ShapeDtypeStruct((rows, 3 * w), bf16), jax.ShapeDtypeStruct((rows, LANE), f32)],
        scratch_shapes=[pltpu.VMEM((tm + 16, w), f32)],
        compiler_params=_cparams(("arbitrary", "arbitrary")),
        name="gdn_prep",
    )(proj, proj, proj, w_conv, pab, pad(a_log), pad(dt_bias))


def _gdn_chunk(direction, h, q_ref, k_ref, v_ref, gbv, dec_all, tot_all, state, masks):
    c = GDN_CHUNK
    incl, strict, eye, lane = masks
    lg = direction * B_HEADS + h
    lb = 2 * B_HEADS + lg
    dec = dec_all[:, lg:lg + 1]
    beta = gbv[:, lb:lb + 1]
    tot = tot_all[:, lg:lg + 1]
    dh, dm, dl = (part.astype(f32) for part in _split3(dec))
    lhs = jnp.where(lane == 0, dh, jnp.where(lane == 1, dm, jnp.where(lane == 2, dl, jnp.where(lane < 6, 1.0, 0.0))))
    rhs = jnp.where(lane < 3, 1.0, jnp.where(lane == 3, -dh, jnp.where(lane == 4, -dm,
                                                                       jnp.where(lane == 5, -dl, 0.0))))
    diff = _dot_nt(lhs.astype(bf16), rhs.astype(bf16))
    gamma = jnp.exp(jnp.where(incl, diff, NEG))
    sl = slice(h * LANE, (h + 1) * LANE)
    k_b = k_ref[:, sl]
    q = q_ref[:, sl].astype(f32)
    k = k_b.astype(f32)
    v = v_ref[:, sl].astype(f32)
    kbeta = k * beta
    kk = _dot_nt(jnp.concatenate([kbeta, q], axis=0).astype(bf16), k_b)
    a_str = jnp.where(strict, kk[:c] * gamma, 0.0)
    a_qk = kk[c:] * gamma
    x = eye - a_str
    a_b = a_str.astype(bf16)
    pw = _dot(a_b, a_b)
    n_sq = int(math.log2(c)) - 2
    for _ in range(n_sq):
        xp = _dot(jnp.concatenate([x, pw], axis=0).astype(bf16), pw.astype(bf16))
        x = x + xp[:c]
        pw = xp[c:]
    x = x + _dot(x.astype(bf16), pw.astype(bf16))
    edec = jnp.exp(dec)
    sol = _dot(x.astype(bf16), jnp.concatenate([v * beta, kbeta * edec], axis=1).astype(bf16))
    u, w = sol[:, :B_DV], sol[:, B_DV:]
    ws = _dot(jnp.concatenate([w, q * edec], axis=0).astype(bf16), state.astype(bf16))
    v_new = u - ws[:c]
    o = ws[c:] + _dot(a_qk.astype(bf16), v_new.astype(bf16))
    kd = k * jnp.exp(tot - dec)
    new_state = state * jnp.exp(tot) + _dot_tn(kd.astype(bf16), v_new.astype(bf16))
    return o, new_state


def _gdn_scan_kernel(qf_ref, kf_ref, vf_ref, gf_ref, qb_ref, kb_ref, vb_ref, gb_ref, of_ref, ob_ref, sf_scr, sb_scr):
    c = GDN_CHUNK

    @pl.when(pl.program_id(1) == 0)
    def _():
        sf_scr[...] = jnp.zeros_like(sf_scr)
        sb_scr[...] = jnp.zeros_like(sb_scr)

    ri = lax.broadcasted_iota(i32, (c, c), 0)
    ci = lax.broadcasted_iota(i32, (c, c), 1)
    lane = lax.broadcasted_iota(i32, (c, LANE), 1)
    eye = (ri == ci).astype(f32)
    dirs = ((qf_ref, kf_ref, vf_ref, gf_ref, of_ref, sf_scr), (qb_ref, kb_ref, vb_ref, gb_ref, ob_ref, sb_scr))
    for direction, (q_ref, k_ref, v_ref, g_ref, o_ref, s_scr) in enumerate(dirs):
        incl = (ri >= ci) if direction == 0 else (ri <= ci)
        strict = (ri > ci) if direction == 0 else (ri < ci)
        tri = incl.astype(bf16)
        gbv = g_ref[...]
        hi, mid, lo = _split3(gbv)
        dec_all = _dot(tri, hi) + _dot(tri, mid) + _dot(tri, lo)
        tot_all = dec_all[c - 1:c, :] if direction == 0 else dec_all[0:1, :]
        for h in range(B_HEADS):
            o, new_state = _gdn_chunk(direction, h, q_ref, k_ref, v_ref, gbv, dec_all, tot_all, s_scr[h],
                                      (incl, strict, eye, lane))
            o_ref[:, h * LANE:(h + 1) * LANE] = o
            s_scr[h] = new_state


def _gdn_scan(qkvn, gb, st):
    batch, nl, rows = st["batch"], st["nl"], st["rows"]
    c = GDN_CHUNK
    ncc, nlc = st["ctx_len"] // c, st["seq"] // c
    w = B_HEADS * LANE

    def fwd(b, s):
        return jnp.where(s < ncc, nl // c + b * ncc + s, b * nlc + (s - ncc))

    def bwd(b, s):
        return jnp.where(s < ncc, nl // c + b * ncc + (ncc - 1 - s), b * nlc + (nlc - 1 - (s - ncc)))

    def specs(rowf):
        return [pl.BlockSpec((c, w), lambda b, s, j=j: (rowf(b, s), j)) for j in range(3)] + \
               [pl.BlockSpec((c, LANE), lambda b, s: (rowf(b, s), 0))]

    return pl.pallas_call(
        _gdn_scan_kernel,
        grid=(batch, ncc + nlc),
        in_specs=specs(fwd) + specs(bwd),
        out_specs=[pl.BlockSpec((c, w), lambda b, s: (fwd(b, s), 0)), pl.BlockSpec((c, w), lambda b, s: (bwd(b, s), 0))],
        out_shape=[jax.ShapeDtypeStruct((rows, w), f32), jax.ShapeDtypeStruct((rows, w), f32)],
        scratch_shapes=[pltpu.VMEM((B_HEADS, B_DK, B_DV), f32), pltpu.VMEM((B_HEADS, B_DK, B_DV), f32)],
        compiler_params=_cparams(("arbitrary", "arbitrary")),
        name="gdn_scan",
    )(qkvn, qkvn, qkvn, gb, qkvn, qkvn, qkvn, gb)


def _gdn_out_kernel(of_ref, ob_ref, z_ref, gain_ref, y_ref):
    o = of_ref[...] + ob_ref[...]
    gain = gain_ref[...]
    for h in range(B_HEADS):
        sl = slice(h * LANE, (h + 1) * LANE)
        oh = o[:, sl]
        y = oh * lax.rsqrt(jnp.mean(oh * oh, axis=-1, keepdims=True) + NORM_EPS) * gain
        z = z_ref[:, sl].astype(f32)
        y_ref[:, sl] = (y * (z * jax.nn.sigmoid(z))).astype(bf16)


def _gdn_out(o_f, o_b, proj, gain, st):
    rows, tm = st["rows"], st["tm_big"]
    w = B_HEADS * LANE
    z_blk = (3 * st["d"] + REL_BZ) // w
    spec = pl.BlockSpec((tm, w), lambda i: (i, 0))
    return pl.pallas_call(
        _gdn_out_kernel,
        grid=(rows // tm,),
        in_specs=[spec, spec, pl.BlockSpec((tm, w), lambda i: (i, z_blk)), pl.BlockSpec((1, B_DV), lambda i: (0, 0))],
        out_specs=spec,
        out_shape=jax.ShapeDtypeStruct((rows, w), bf16),
        compiler_params=_cparams(("arbitrary",)),
        name="gdn_out",
    )(o_f, o_b, proj, gain.reshape(1, B_DV))


def _mixer_b(proj, pab, w_conv, a_log, dt_bias, out_gain, st):
    qkvn, gb = _gdn_prep(proj, pab, w_conv, a_log, dt_bias, st)
    o_f, o_b = _gdn_scan(qkvn, gb, st)
    return _gdn_out(o_f, o_b, proj, out_gain, st)


def _merge_kernel(ya_ref, yb_ref, yc_ref, wa_ref, wb_ref, wc_ref, ga_ref, gb_ref, gc_ref, o_ref):
    acc = None
    for y_ref, w_ref, g_ref in ((ya_ref, wa_ref, ga_ref), (yb_ref, wb_ref, gb_ref), (yc_ref, wc_ref, gc_ref)):
        t = jax.nn.sigmoid(g_ref[...].astype(f32)) * _dot(y_ref[...], w_ref[...])
        acc = t if acc is None else acc + t
    o_ref[...] = acc.astype(bf16)


def _merge(ya, yb, yc, w_branch_b, proj, st, rows):
    d, tm = st["d"], st["tm_big"]
    tn = _pick((1024, 512, 256, 128), d)
    bw = ya.shape[1]
    nj = d // tn
    y_spec = pl.BlockSpec((tm, bw), lambda i, j: (i, 0))
    w_specs = [pl.BlockSpec((bw, tn), lambda i, j, r=r: (r, j)) for r in range(3)]
    g_specs = [pl.BlockSpec((tm, tn), lambda i, j, r=r: (i, r * nj + j)) for r in range(3)]
    return pl.pallas_call(
        _merge_kernel,
        grid=(rows // tm, nj),
        in_specs=[y_spec, y_spec, y_spec] + w_specs + g_specs,
        out_specs=pl.BlockSpec((tm, tn), lambda i, j: (i, j)),
        out_shape=jax.ShapeDtypeStruct((rows, d), bf16),
        compiler_params=_cparams(("arbitrary", "arbitrary")),
        name="branch_merge",
    )(ya, yb, yc, w_branch_b, w_branch_b, w_branch_b, proj, proj, proj)


def _out_proj_kernel(m_ref, w_ref, x_ref, g_ref, o_ref):
    o_ref[...] = x_ref[...] + g_ref[0] * _dot(m_ref[...], w_ref[...])


def _out_proj(mixed, w_o_b, xall, gate, st, rows):
    d, tm = st["d"], st["tm_big"]
    tn = _pick((1024, 512, 256, 128), d)
    group = _group_of_tile(st, tm)
    return pl.pallas_call(
        _out_proj_kernel,
        grid=(rows // tm, d // tn),
        in_specs=[pl.BlockSpec((tm, d), lambda i, j: (i, 0)),
                  pl.BlockSpec((d, tn), lambda i, j: (0, j)),
                  pl.BlockSpec((tm, tn), lambda i, j: (i, j)),
                  pl.BlockSpec((1, 1, tn), lambda i, j: (group(i), 0, j))],
        out_specs=pl.BlockSpec((tm, tn), lambda i, j: (i, j)),
        out_shape=jax.ShapeDtypeStruct((rows, d), f32),
        compiler_params=_cparams(("arbitrary", "arbitrary")),
        name="out_proj",
    )(mixed, w_o_b, xall, gate)


def _norm_router_kernel(x_ref, s_ref, sh_ref, wr_ref, br_ref, h_ref, aff_ref):
    x = x_ref[...]
    h = x * lax.rsqrt(jnp.mean(x * x, axis=-1, keepdims=True) + NORM_EPS) * s_ref[0] + sh_ref[0]
    h_ref[...] = h
    logits = _dot_nt(wr_ref[...], h.astype(bf16)) + br_ref[...]
    e = jnp.exp(logits - jnp.max(logits, axis=0, keepdims=True))
    aff_ref[...] = e / jnp.sum(e, axis=0, keepdims=True)


def _norm_router(x, lay, w_router, b_router, st, rows):
    d, tm = st["d"], st["tm_big"]
    group = _group_of_tile(st, tm)
    ne = w_router.shape[1]
    return pl.pallas_call(
        _norm_router_kernel,
        grid=(rows // tm,),
        in_specs=[pl.BlockSpec((tm, d), lambda i: (i, 0)),
                  pl.BlockSpec((1, 1, d), lambda i: (group(i), 0, 0)),
                  pl.BlockSpec((1, 1, d), lambda i: (group(i), 0, 0)),
                  pl.BlockSpec((ne, d), lambda i: (0, 0)),
                  pl.BlockSpec((ne, 1), lambda i: (0, 0))],
        out_specs=[pl.BlockSpec((tm, d), lambda i: (i, 0)), pl.BlockSpec((ne, tm), lambda i: (0, i))],
        out_shape=[jax.ShapeDtypeStruct((rows, d), f32), jax.ShapeDtypeStruct((ne, rows), f32)],
        compiler_params=_cparams(("arbitrary",)),
        name="norm_router",
    )(x, lay["scale2"], lay["shift2"], w_router.T.astype(bf16), b_router.reshape(ne, 1).astype(f32))


PREFIX_W = 256


def _prefix_excl(mask):
    ne, t = mask.shape
    w = min(PREFIX_W, t)
    r = lax.broadcasted_iota(i32, (w, w), 0)
    c = lax.broadcasted_iota(i32, (w, w), 1)
    upper = jnp.where(r < c, 1.0, 0.0).astype(bf16)
    off = jnp.zeros((ne, 1), f32)
    outs = []
    for j in range(t // w):
        xb = jnp.where(mask[:, j * w:(j + 1) * w], 1.0, 0.0)
        outs.append(_dot(xb.astype(bf16), upper) + off)
        off = off + jnp.sum(xb, axis=1, keepdims=True)
    return jnp.concatenate(outs, axis=1) if len(outs) > 1 else outs[0]


def _topk_kernel(aff_ref, idx_ref, gate_ref, *, t, cap, base0):
    g = pl.program_id(0)
    aff = aff_ref[...]
    ne = aff.shape[0]
    bits = pltpu.bitcast(aff, i32)
    thr = jnp.zeros((ne, 1), i32)
    for bit in range(30, -1, -1):
        cand = thr | (1 << bit)
        cnt = jnp.sum(jnp.where(bits >= cand, 1.0, 0.0), axis=1, keepdims=True)
        thr = jnp.where(cnt >= cap, cand, thr)
    gt = bits > thr
    eq = bits == thr
    need = cap - jnp.sum(jnp.where(gt, 1.0, 0.0), axis=1, keepdims=True)
    sel = gt | (eq & (_prefix_excl(eq) < need))
    pos = _prefix_excl(sel).astype(i32)
    tok = lax.broadcasted_iota(i32, (1, t), 1)
    t_hi = (tok >> 8).astype(f32)
    t_lo = (tok & 255).astype(f32)
    slot = lax.broadcasted_iota(i32, (cap, t), 0)
    base = base0 + g * t
    for e in range(ne):
        onehot = jnp.where((slot == pos[e:e + 1, :]) & sel[e:e + 1, :], 1.0, 0.0).astype(bf16)
        g_hi, g_mid, g_lo = _split3(aff[e:e + 1, :])
        vals = jnp.concatenate([t_hi.astype(bf16), t_lo.astype(bf16), g_hi, g_mid, g_lo,
                                jnp.zeros((3, t), bf16)], axis=0)
        out = _dot_nt(vals, onehot)
        idx_ref[0, e:e + 1, :] = base + (out[0:1] * 256.0 + out[1:2]).astype(i32)
        gate_ref[0, e:e + 1, :] = out[2:3] + out[3:4] + out[4:5]


def _topk(aff_t, n_groups, t, cap, col_blk0, base0):
    ne = aff_t.shape[0]
    return pl.pallas_call(
        functools.partial(_topk_kernel, t=t, cap=cap, base0=base0),
        grid=(n_groups,),
        in_specs=[pl.BlockSpec((ne, t), lambda g: (0, col_blk0 + g))],
        out_specs=[pl.BlockSpec((1, ne, cap), lambda g: (g, 0, 0)), pl.BlockSpec((1, ne, cap), lambda g: (g, 0, 0))],
        out_shape=[jax.ShapeDtypeStruct((n_groups, ne, cap), i32), jax.ShapeDtypeStruct((n_groups, ne, cap), f32)],
        compiler_params=_cparams(("arbitrary",)),
        name="expert_topk",
    )(aff_t)


BF16_ROWS = 16


def _pieces(m, want):
    for n in range(want, 0, -1):
        if m % (n * BF16_ROWS) == 0:
            return n
    raise ValueError(m)


def _ffn1_kernel(idx_ref, h_hbm, wg_ref, wu_ref, o_ref, stage, xs, sem, *, m):
    chunk = m // _pieces(m, 2)

    @pl.when(pl.program_id(1) == 0)
    def _():
        for c0 in range(0, m, chunk):
            def start(r, carry, c0=c0):
                row = idx_ref[0, 0, c0 + r]
                pltpu.make_async_copy(h_hbm.at[pl.ds(row, 1), :], stage.at[pl.ds(r, 1), :], sem).start()
                return carry

            def wait(r, carry):
                pltpu.make_async_copy(h_hbm.at[pl.ds(0, 1), :], stage.at[pl.ds(r, 1), :], sem).wait()
                return carry

            lax.fori_loop(0, chunk, start, 0)
            lax.fori_loop(0, chunk, wait, 0)
            xs[c0:c0 + chunk, :] = stage[...].astype(bf16)

    wg = wg_ref[0].astype(bf16)
    wu = wu_ref[0].astype(bf16)
    rc = m // _pieces(m, 4)
    for r0 in range(0, m, rc):
        xr = xs[r0:r0 + rc, :]
        a = _dot(xr, wg)
        o_ref[0, r0:r0 + rc, :] = (a * jax.nn.sigmoid(a) * _dot(xr, wu)).astype(bf16)


def _ffn1(idx, h2, w_gate, w_up):
    ne, m = idx.shape
    d, ff = w_gate.shape[1], w_gate.shape[2]
    tf = _pick((512, 256, 128), ff)
    return pl.pallas_call(
        functools.partial(_ffn1_kernel, m=m),
        grid=(ne, ff // tf),
        in_specs=[pl.BlockSpec((1, 1, m), lambda e, f: (e, 0, 0), memory_space=pltpu.SMEM),
                  pl.BlockSpec(memory_space=pl.ANY),
                  pl.BlockSpec((1, d, tf), lambda e, f: (e, 0, f)),
                  pl.BlockSpec((1, d, tf), lambda e, f: (e, 0, f))],
        out_specs=pl.BlockSpec((1, m, tf), lambda e, f: (e, 0, f)),
        out_shape=jax.ShapeDtypeStruct((ne, m, ff), bf16),
        scratch_shapes=[pltpu.VMEM((m // _pieces(m, 2), d), f32), pltpu.VMEM((m, d), bf16),
                        pltpu.SemaphoreType.DMA(())],
        compiler_params=_cparams(("arbitrary", "arbitrary")),
        name="expert_ffn_in",
    )(idx.reshape(ne, 1, m), h2, w_gate, w_up)


def _ffn2_kernel(h_ref, w_ref, g_ref, o_ref):
    o_ref[0] = _dot(h_ref[0], w_ref[0].astype(bf16)) * g_ref[0]


def _ffn2(hid, w_down, gates):
    ne, m, ff = hid.shape
    d = w_down.shape[2]
    td = _pick((512, 256, 128), d)
    return pl.pallas_call(
        _ffn2_kernel,
        grid=(ne, d // td),
        in_specs=[pl.BlockSpec((1, m, ff), lambda e, j: (e, 0, 0)),
                  pl.BlockSpec((1, ff, td), lambda e, j: (e, 0, j)),
                  pl.BlockSpec((1, m, 1), lambda e, j: (e, 0, 0))],
        out_specs=pl.BlockSpec((1, m, td), lambda e, j: (e, 0, j)),
        out_shape=jax.ShapeDtypeStruct((ne, m, d), f32),
        compiler_params=_cparams(("arbitrary", "arbitrary")),
        name="expert_ffn_out",
    )(hid, w_down, gates.reshape(ne, m, 1))


def _combine_kernel(*refs, cap, aliased):
    idx_ref, y_ref = refs[0], refs[1]
    o_ref = refs[-1]

    @pl.when(pl.program_id(2) == 0)
    def _():
        o_ref[...] = jnp.zeros_like(o_ref)

    def body(c, carry):
        r = idx_ref[0, 0, c]
        o_ref[pl.ds(r, 1), :] = o_ref[pl.ds(r, 1), :] + y_ref[0, pl.ds(c, 1), :]
        return carry

    lax.fori_loop(0, cap, body, 0, unroll=8)


def _combine(local_idx, y, n_groups, t, cap, y_blk0, out_blk0, out_rows, prev=None):
    ne, _, d = y.shape
    td = _pick((1024, 512, 256, 128), d)
    in_specs = [pl.BlockSpec((1, 1, cap), lambda g, j, e: (e * n_groups + g, 0, 0), memory_space=pltpu.SMEM),
                pl.BlockSpec((1, cap, td), lambda g, j, e: (e, y_blk0 + g, j))]
    args = [local_idx.reshape(ne * n_groups, 1, cap), y]
    aliases = {}
    if prev is not None:
        in_specs.append(pl.BlockSpec(memory_space=pl.ANY))
        args.append(prev)
        aliases = {2: 0}
    return pl.pallas_call(
        functools.partial(_combine_kernel, cap=cap, aliased=prev is not None),
        grid=(n_groups, d // td, ne),
        in_specs=in_specs,
        out_specs=pl.BlockSpec((t, td), lambda g, j, e: (out_blk0 + g, j)),
        out_shape=jax.ShapeDtypeStruct((out_rows, d), f32),
        input_output_aliases=aliases,
        compiler_params=_cparams(("arbitrary", "arbitrary", "arbitrary")),
        name="expert_combine",
    )(*args)


def _moe(x_mid, lay, w_router, b_router, w_gate, w_up, w_down, st, with_ctx):
    batch, seq, ctx_len, nl = st["batch"], st["seq"], st["ctx_len"], st["nl"]
    ne = w_router.shape[1]
    rows = nl + (st["nc"] if with_ctx else 0)
    h2, aff_t = _norm_router(x_mid, lay, w_router, b_router, st, rows)
    cap = EC_CAPACITY * seq // ne
    idx_l, gate_l = _topk(aff_t, batch, seq, cap, 0, 0)
    idx = jnp.swapaxes(idx_l, 0, 1).reshape(ne, batch * cap)
    gates = jnp.swapaxes(gate_l, 0, 1).reshape(ne, batch * cap)
    loc_l = jnp.swapaxes(idx_l - (jnp.arange(batch, dtype=i32) * seq)[:, None, None], 0, 1)
    if with_ctx:
        cap_c = EC_CAPACITY * ctx_len // ne
        idx_c, gate_c = _topk(aff_t, batch, ctx_len, cap_c, nl // ctx_len, nl)
        idx = jnp.concatenate([idx, jnp.swapaxes(idx_c, 0, 1).reshape(ne, batch * cap_c)], axis=1)
        gates = jnp.concatenate([gates, jnp.swapaxes(gate_c, 0, 1).reshape(ne, batch * cap_c)], axis=1)
        loc_c = jnp.swapaxes(idx_c - (nl + jnp.arange(batch, dtype=i32) * ctx_len)[:, None, None], 0, 1)
    hid = _ffn1(idx, h2, w_gate, w_up)
    y = _ffn2(hid, w_down, gates)
    out = _combine(loc_l, y, batch, seq, cap, 0, 0, rows)
    if with_ctx:
        out = _combine(loc_c, y, batch, ctx_len, cap_c, batch * cap // cap_c, nl // ctx_len, rows, prev=out)
    return out


def _pack_w_in(w):
    d = w.shape[0]
    sizes = (A_HEADS * HEAD_DIM, A_KV_HEADS * HEAD_DIM, A_KV_HEADS * HEAD_DIM,
             B_HEADS * (2 * B_DK + B_DV), B_HEADS * B_DV, 2 * B_HEADS, 2 * B_HEADS,
             C_Q_RANK, C_KV_RANK, C_ROPE, 3 * d)
    aq, ak, av, bqkv, bz, ba, bb, cq, ckv, ckr, gt = jnp.split(w, np.cumsum(sizes)[:-1].tolist(), axis=1)
    z = lambda n: jnp.zeros((d, n), w.dtype)
    packed = jnp.concatenate(
        [gt, cq, ckv, ckr, z(LANE - C_ROPE), ba, bb, z(LANE - 4 * B_HEADS), aq, ak, av, bqkv, bz], axis=1)
    assert packed.shape[1] == 3 * d + REL_END
    return packed.astype(bf16)


def _pack_cq_up(w):
    r = w.shape[0]
    w3 = w.reshape(r, C_HEADS, C_NOPE + C_ROPE)
    w3 = jnp.concatenate([w3, jnp.zeros((r, C_HEADS, 2 * LANE - C_NOPE - C_ROPE), w.dtype)], axis=2)
    return w3.reshape(r, C_HEADS * 2 * LANE).astype(bf16)


def _pack_ckv_up(w):
    r = w.shape[0]
    w3 = w.reshape(r, C_HEADS, C_NOPE + C_V)
    return jnp.concatenate([w3[:, :, :C_NOPE].reshape(r, -1), w3[:, :, C_NOPE:].reshape(r, -1)], axis=1).astype(bf16)


def _prepare_statics(batch, seq, ctx_len, d):
    nl, nc = batch * seq, batch * ctx_len
    assert seq % GRID_W == 0 and seq % WBLK == 0 and ctx_len % WBLK == 0 and nl % ctx_len == 0
    st = dict(batch=batch, seq=seq, ctx_len=ctx_len, d=d, nl=nl, nc=nc, rows=nl + nc)
    st["tm_big"] = _pick((1024, 512, 256, 128), seq, nc)
    st["tm_proj"] = _pick((512, 256, 128), seq, nc)
    st["tm_seq"] = _pick((256, 128, 64), seq, ctx_len)
    st["tq"] = _pick((256, 128), seq, ctx_len)
    st["tabs_a"] = _rope_tables(batch, seq, nc, HEAD_DIM)
    st["tabs_c"] = _rope_tables(batch, seq, nc, C_ROPE)
    return st


def _group_of_tile(st, tm):
    per_seq = st["seq"] // tm
    batch = st["batch"]
    return lambda i: jnp.minimum(i // per_seq, batch)


def _layer_mods(mod_l, batch, g1, g2):
    sh1, sc1, gt1, sh2, sc2, gt2 = jnp.split(mod_l[:batch + 1], 6, axis=-1)
    r3 = lambda a: a[:, None, :]
    return dict(scale1=r3(g1[None, :] * (1.0 + sc1)), shift1=r3(sh1), gate1=r3(gt1),
                scale2=r3(g2[None, :] * (1.0 + sc2)), shift2=r3(sh2), gate2=r3(gt2))


def _proj_in(xall, lay, w_in_p, st, delta=None, gate=None):
    d = st["d"]
    tm = st["tm_proj"]
    tn = _pick((1280, 768, 640, 384, 256, 128), w_in_p.shape[1])
    return _norm_matmul(xall, 0, d, lay["scale1"], lay["shift1"], w_in_p, _group_of_tile(st, tm), tm, tn,
                        delta=delta, gate=gate, side_col=3 * d + REL_AB, name="proj_in")


def _mixer_a(proj, sink, st, with_ctx):
    d3 = 3 * st["d"]
    aqk = _prep_a(proj, st["tabs_a"], (d3 + REL_AQ) // ((A_HEADS + A_KV_HEADS) * LANE), st["tm_big"])
    return _win_attn(aqk, proj, (d3 + REL_AV) // (A_KV_HEADS * LANE), sink, st["batch"], st["seq"], st["ctx_len"],
                     with_ctx)


def _mixer_c(proj, q_gain, cq_up_p, kv_gain, ckv_up_p, st, with_ctx):
    d3 = 3 * st["d"]
    tm = st["tm_big"]
    one = lambda i: 0
    qp = _norm_matmul(proj, (d3 + REL_CQ) // C_Q_RANK, C_Q_RANK, q_gain.reshape(1, 1, -1),
                      jnp.zeros((1, 1, C_Q_RANK), f32), cq_up_p, one, tm, cq_up_p.shape[1], name="mla_q_up")[0]
    kvp = _norm_matmul(proj, (d3 + REL_CKV) // C_KV_RANK, C_KV_RANK, kv_gain.reshape(1, 1, -1),
                       jnp.zeros((1, 1, C_KV_RANK), f32), ckv_up_p, one, tm, ckv_up_p.shape[1], name="mla_kv_up")[0]
    cq, ck = _prep_c(qp, kvp, proj, st["tabs_c"], (d3 + REL_CKR) // LANE, tm)
    return _mla_attn(cq, ck, kvp, st["batch"], st["seq"], st["ctx_len"], st["tq"], with_ctx)


def kernel(x, c, ctx, c_ctx, w_mod, b_mod, g_norm1, g_norm2, w_in, a_sink, b_conv, b_a_log, b_dt_bias, b_out_gain,
           c_q_gain, c_q_up, c_kv_gain, c_kv_up, w_branch, w_o, w_router, b_router, w_exp_gate, w_exp_up,
           w_exp_down, g_final):
    batch, seq, d = x.shape
    ctx_len = ctx.shape[1]
    depth = w_in.shape[0]
    assert batch + 1 <= 8
    st = _prepare_statics(batch, seq, ctx_len, d)
    nl, rows_all = st["nl"], st["rows"]
    c_all = jnp.concatenate([c, c_ctx[None], jnp.zeros((8 - batch - 1, d), f32)], axis=0)
    mods = _mod(c_all, w_mod, b_mod)
    xall = jnp.concatenate([x.reshape(nl, d), ctx.reshape(batch * ctx_len, d)], axis=0)
    delta = gate_prev = None
    for l in range(depth):
        need_ctx = l < depth - 1
        lay = _layer_mods(mods[l], batch, g_norm1[l], g_norm2[l])
        w_in_p = _pack_w_in(w_in[l])
        if delta is None:
            proj, pab = _proj_in(xall, lay, w_in_p, st)
        else:
            proj, xall, pab = _proj_in(xall, lay, w_in_p, st, delta=delta, gate=gate_prev)
        ya = _mixer_a(proj, a_sink[l], st, need_ctx)
        yb = _mixer_b(proj, pab, b_conv[l], b_a_log[l], b_dt_bias[l], b_out_gain[l], st)
        yc = _mixer_c(proj, c_q_gain[l], _pack_cq_up(c_q_up[l]), c_kv_gain[l], _pack_ckv_up(c_kv_up[l]), st, need_ctx)
        rows = rows_all if need_ctx else nl
        mixed = _merge(ya, yb, yc, w_branch[l].astype(bf16), proj, st, rows)
        xall = _out_proj(mixed, w_o[l].astype(bf16), xall, lay["gate1"], st, rows)
        delta = _moe(xall, lay, w_router[l], b_router[l], w_exp_gate[l], w_exp_up[l], w_exp_down[l], st, need_ctx)
        gate_prev = lay["gate2"]
    tm = st["tq"]
    out = _final_norm(xall, delta, gate_prev, g_final, _group_of_tile(st, tm), tm, nl)
    return out.reshape(batch, seq, d)
```

```python
import functools
import math

import numpy as np
import jax
import jax.numpy as jnp
from jax import lax
from jax.experimental import pallas as pl
from jax.experimental.pallas import tpu as pltpu

f32 = jnp.float32
bf16 = jnp.bfloat16
i32 = jnp.int32

GRID_W = 64
HEAD_DIM = 128
ROPE_BASE = 10000.0
NORM_EPS = 1e-6
A_HEADS = 6
A_KV_HEADS = 2
A_GROUP = A_HEADS // A_KV_HEADS
WINDOW = 128
WBLK = 128
B_HEADS = 6
B_DK = 128
B_DV = 128
CONV_K = 5
GDN_CHUNK = 64
C_HEADS = 6
C_Q_RANK = 512
C_KV_RANK = 256
C_NOPE = 128
C_ROPE = 64
C_V = 128
N_EXPERTS = 16
EC_CAPACITY = 2

LANE = 128
NEG = -1e30
VMEM_LIMIT = 56 * 1024 * 1024

COL_GT = 0
REL_CQ = 0
REL_CKV = 512
REL_CKR = 768
REL_AB = 896
REL_AQ = 1024
REL_AK = 1792
REL_AV = 2048
REL_BQKV = 2304
REL_BZ = 4608
REL_END = 5376


def _cparams(sem):
    return pltpu.CompilerParams(dimension_semantics=sem, vmem_limit_bytes=VMEM_LIMIT)


def _pick(cands, *ns):
    for c in cands:
        if all(n % c == 0 for n in ns):
            return c
    raise ValueError(f"no tile in {cands} divides {ns}")


def _dot(a, b):
    return jnp.dot(a, b, preferred_element_type=f32)


def _dot_nt(a, b):
    return lax.dot_general(a, b, (((1,), (1,)), ((), ())), preferred_element_type=f32)


def _dot_tn(a, b):
    return lax.dot_general(a, b, (((0,), (0,)), ((), ())), preferred_element_type=f32)


def _split3(x):
    hi = x.astype(bf16)
    r = x - hi.astype(f32)
    mid = r.astype(bf16)
    lo = (r - mid.astype(f32)).astype(bf16)
    return hi, mid, lo


def _mod_kernel(c_ref, w_ref, b_ref, o_ref):
    c = c_ref[...]
    sc = (c * jax.nn.sigmoid(c)).astype(bf16)
    o_ref[0] = _dot(sc, w_ref[0].astype(bf16)) + b_ref[0]


def _mod(c_all, w_mod, b_mod):
    n_layers, d, n = w_mod.shape
    tn = _pick((1024, 512, 256, 128), n)
    return pl.pallas_call(
        _mod_kernel,
        grid=(n_layers, n // tn),
        in_specs=[
            pl.BlockSpec((8, d), lambda l, j: (0, 0)),
            pl.BlockSpec((1, d, tn), lambda l, j: (l, 0, j)),
            pl.BlockSpec((1, 1, tn), lambda l, j: (l, 0, j)),
        ],
        out_specs=pl.BlockSpec((1, 8, tn), lambda l, j: (l, 0, j)),
        out_shape=jax.ShapeDtypeStruct((n_layers, 8, n), f32),
        compiler_params=_cparams(("arbitrary", "arbitrary")),
        name="adaln_mod",
    )(c_all, w_mod, b_mod.reshape(n_layers, 1, n))


def _norm_matmul_kernel(*refs, has_res, side, eps):
    it = iter(refs)
    x_ref = next(it)
    d_ref = g_ref = None
    if has_res:
        d_ref, g_ref = next(it), next(it)
    s_ref, sh_ref, w_ref, o_ref = next(it), next(it), next(it), next(it)
    xn_ref = next(it) if has_res else None
    side_ref = next(it) if side is not None else None
    h_scr = next(it)
    j = pl.program_id(1)

    @pl.when(j == 0)
    def _():
        x = x_ref[...].astype(f32)
        if has_res:
            x = x + g_ref[0] * d_ref[...].astype(f32)
            xn_ref[...] = x
        ms = jnp.mean(x * x, axis=-1, keepdims=True)
        h = x * lax.rsqrt(ms + eps) * s_ref[0] + sh_ref[0]
        h_scr[...] = h.astype(bf16)

    acc = _dot(h_scr[...], w_ref[...])
    o_ref[...] = acc.astype(o_ref.dtype)
    if side is not None:
        side_j, side_off = side

        @pl.when(j == side_j)
        def _():
            side_ref[...] = acc[:, side_off:side_off + LANE]


def _norm_matmul(x, x_col, k, scale, shift, w, group_of_tile, tm, tn, *, rows=None, delta=None, gate=None,
                 side_col=None, name):
    rows = x.shape[0] if rows is None else rows
    n = w.shape[1]
    assert rows % tm == 0 and n % tn == 0 and w.shape[0] == k
    has_res = delta is not None
    side = None
    if side_col is not None:
        side = (side_col // tn, side_col % tn)
        assert side[1] + LANE <= tn
    in_specs = [pl.BlockSpec((tm, k), lambda i, j: (i, x_col))]
    args = [x]
    if has_res:
        in_specs += [pl.BlockSpec((tm, k), lambda i, j: (i, 0)),
                     pl.BlockSpec((1, 1, k), lambda i, j: (group_of_tile(i), 0, 0))]
        args += [delta, gate]
    in_specs += [pl.BlockSpec((1, 1, k), lambda i, j: (group_of_tile(i), 0, 0)),
                 pl.BlockSpec((1, 1, k), lambda i, j: (group_of_tile(i), 0, 0)),
                 pl.BlockSpec((k, tn), lambda i, j: (0, j))]
    args += [scale, shift, w]
    out_specs = [pl.BlockSpec((tm, tn), lambda i, j: (i, j))]
    out_shape = [jax.ShapeDtypeStruct((rows, n), bf16)]
    if has_res:
        out_specs.append(pl.BlockSpec((tm, k), lambda i, j: (i, 0)))
        out_shape.append(jax.ShapeDtypeStruct((rows, k), f32))
    if side is not None:
        out_specs.append(pl.BlockSpec((tm, LANE), lambda i, j: (i, 0)))
        out_shape.append(jax.ShapeDtypeStruct((rows, LANE), f32))
    return pl.pallas_call(
        functools.partial(_norm_matmul_kernel, has_res=has_res, side=side, eps=NORM_EPS),
        grid=(rows // tm, n // tn),
        in_specs=in_specs,
        out_specs=out_specs,
        out_shape=out_shape,
        scratch_shapes=[pltpu.VMEM((tm, k), bf16)],
        compiler_params=_cparams(("arbitrary", "arbitrary")),
        name=name,
    )(*args)


def _final_norm_kernel(x_ref, d_ref, g_ref, gain_ref, o_ref, *, eps):
    x = x_ref[...] + g_ref[0] * d_ref[...]
    ms = jnp.mean(x * x, axis=-1, keepdims=True)
    o_ref[...] = x * lax.rsqrt(ms + eps) * gain_ref[...]


def _final_norm(x, delta, gate, gain, group_of_tile, tm, rows):
    d = x.shape[1]
    return pl.pallas_call(
        functools.partial(_final_norm_kernel, eps=NORM_EPS),
        grid=(rows // tm,),
        in_specs=[pl.BlockSpec((tm, d), lambda i: (i, 0)),
                  pl.BlockSpec((tm, d), lambda i: (i, 0)),
                  pl.BlockSpec((1, 1, d), lambda i: (group_of_tile(i), 0, 0)),
                  pl.BlockSpec((1, d), lambda i: (0, 0))],
        out_specs=pl.BlockSpec((tm, d), lambda i: (i, 0)),
        out_shape=jax.ShapeDtypeStruct((rows, d), f32),
        compiler_params=_cparams(("arbitrary",)),
        name="final_norm",
    )(x, delta, gate, gain.reshape(1, d))


def _rope_tables(batch, seq, n_ctx_rows, rot):
    half, quarter = rot // 2, rot // 4
    t = np.arange(seq)
    row, col = (t // GRID_W).astype(np.float32), (t % GRID_W).astype(np.float32)
    inv = (ROPE_BASE ** (-np.arange(quarter, dtype=np.float32) / quarter)).astype(np.float32)
    lane = np.arange(LANE)
    pos = np.where((lane % rot) < half, row[:, None], col[:, None]).astype(np.float32)
    ang = pos * inv[lane % quarter][None, :]
    active = (lane < rot)[None, :]
    cos = np.where(active, np.cos(ang), 1.0)
    sin = np.where(active, np.sin(ang), 0.0)
    first = ((lane % half) < quarter)[None, :]
    s1 = np.where(first, -sin, 0.0)
    s2 = np.where(first, 0.0, sin)

    def full(tab, fill):
        lat = np.tile(tab.astype(np.float32), (batch, 1))
        ctx = np.full((n_ctx_rows, LANE), fill, np.float32)
        return jnp.asarray(np.concatenate([lat, ctx], axis=0))

    return full(cos, 1.0), full(s1, 0.0), full(s2, 0.0)


def _rope(x, cos, s1, s2, quarter):
    return x * cos + pltpu.roll(x, LANE - quarter, 1) * s1 + pltpu.roll(x, quarter, 1) * s2


def _prep_a_kernel(p_ref, cos_ref, s1_ref, s2_ref, o_ref, *, scale):
    cos, s1, s2 = cos_ref[...], s1_ref[...], s2_ref[...]
    for h in range(A_HEADS + A_KV_HEADS):
        sl = slice(h * LANE, (h + 1) * LANE)
        y = _rope(p_ref[:, sl].astype(f32), cos, s1, s2, HEAD_DIM // 4)
        if h < A_HEADS:
            y = y * scale
        o_ref[:, sl] = y.astype(bf16)


def _prep_a(proj, tabs, col_blk, tm):
    rows = proj.shape[0]
    w = (A_HEADS + A_KV_HEADS) * LANE
    tab_spec = pl.BlockSpec((tm, LANE), lambda i: (i, 0))
    return pl.pallas_call(
        functools.partial(_prep_a_kernel, scale=HEAD_DIM ** -0.5),
        grid=(rows // tm,),
        in_specs=[pl.BlockSpec((tm, w), lambda i: (i, col_blk)), tab_spec, tab_spec, tab_spec],
        out_specs=pl.BlockSpec((tm, w), lambda i: (i, 0)),
        out_shape=jax.ShapeDtypeStruct((rows, w), bf16),
        compiler_params=_cparams(("arbitrary",)),
        name="rope_a",
    )(proj, *tabs)


def _prep_c_kernel(q_ref, kn_ref, kr_ref, cos_ref, s1_ref, s2_ref, oq_ref, ok_ref, *, scale):
    cos, s1, s2 = cos_ref[...], s1_ref[...], s2_ref[...]
    quarter = C_ROPE // 4
    kr = _rope(kr_ref[...].astype(f32), cos, s1, s2, quarter).astype(bf16)
    for h in range(C_HEADS):
        a, b, c = 2 * h * LANE, (2 * h + 1) * LANE, (2 * h + 2) * LANE
        oq_ref[:, a:b] = (q_ref[:, a:b].astype(f32) * scale).astype(bf16)
        oq_ref[:, b:c] = (_rope(q_ref[:, b:c].astype(f32), cos, s1, s2, quarter) * scale).astype(bf16)
        ok_ref[:, a:b] = kn_ref[:, h * LANE:(h + 1) * LANE]
        ok_ref[:, b:c] = kr


def _prep_c(qp, kvp, proj, tabs, ckr_blk, tm):
    rows = qp.shape[0]
    w = C_HEADS * 2 * LANE
    tab_spec = pl.BlockSpec((tm, LANE), lambda i: (i, 0))
    return pl.pallas_call(
        functools.partial(_prep_c_kernel, scale=(C_NOPE + C_ROPE) ** -0.5),
        grid=(rows // tm,),
        in_specs=[pl.BlockSpec((tm, w), lambda i: (i, 0)),
                  pl.BlockSpec((tm, C_HEADS * LANE), lambda i: (i, 0)),
                  pl.BlockSpec((tm, LANE), lambda i: (i, ckr_blk)),
                  tab_spec, tab_spec, tab_spec],
        out_specs=[pl.BlockSpec((tm, w), lambda i: (i, 0)), pl.BlockSpec((tm, w), lambda i: (i, 0))],
        out_shape=[jax.ShapeDtypeStruct((rows, w), bf16), jax.ShapeDtypeStruct((rows, w), bf16)],
        compiler_params=_cparams(("arbitrary",)),
        name="rope_c",
    )(qp, kvp, proj, *tabs)


def _softmax_attend(q, pieces, sink_col=None):
    scores = []
    for k, _, mask in pieces:
        s = _dot_nt(q, k)
        if mask is not None:
            s = jnp.where(mask, s, NEG)
        scores.append(s)
    m = None
    for s in scores:
        ms = jnp.max(s, axis=-1, keepdims=True)
        m = ms if m is None else jnp.maximum(m, ms)
    if sink_col is not None:
        m = jnp.maximum(m, sink_col)
    l = None
    o = None
    for s, (_, v, _) in zip(scores, pieces):
        p = jnp.exp(s - m)
        ls = jnp.sum(p, axis=-1, keepdims=True)
        os_ = _dot(p.astype(bf16), v)
        l = ls if l is None else l + ls
        o = os_ if o is None else o + os_
    if sink_col is not None:
        l = l + jnp.exp(sink_col - m)
    return o / l


def _win_attn_kernel(sink_ref, q_ref, kp_ref, kc_ref, kn_ref, vp_ref, vc_ref, vn_ref, kx_ref, vx_ref, o_ref, *,
                     nb, seq):
    n = pl.program_id(1)
    is_lat = n < nb
    g3 = A_GROUP * WBLK
    r = lax.broadcasted_iota(i32, (g3, 3 * WBLK), 0)
    c = lax.broadcasted_iota(i32, (g3, 3 * WBLK), 1)
    qpos = n * WBLK + (r % WBLK)
    kpos = (n - 1) * WBLK + c
    band = (jnp.abs(kpos - qpos) <= WINDOW) & (kpos >= 0) & (kpos < seq) & is_lat
    rcol = lax.broadcasted_iota(i32, (g3, 1), 0)
    for hk in range(A_KV_HEADS):
        ks = slice(hk * LANE, (hk + 1) * LANE)
        q = jnp.concatenate(
            [q_ref[:, (hk * A_GROUP + g) * LANE:(hk * A_GROUP + g + 1) * LANE] for g in range(A_GROUP)], axis=0)
        kb = jnp.concatenate([kp_ref[:, ks], kc_ref[:, ks], kn_ref[:, ks]], axis=0)
        vb = jnp.concatenate([vp_ref[:, ks], vc_ref[:, ks], vn_ref[:, ks]], axis=0)
        sink_col = jnp.zeros((g3, 1), f32)
        for g in range(A_GROUP):
            sink_col = jnp.where(rcol // WBLK == g, sink_ref[hk * A_GROUP + g], sink_col)
        o = _softmax_attend(q, [(kb, vb, band), (kx_ref[:, ks], vx_ref[:, ks], None)], sink_col)
        for g in range(A_GROUP):
            h = hk * A_GROUP + g
            o_ref[:, h * LANE:(h + 1) * LANE] = o[g * WBLK:(g + 1) * WBLK].astype(bf16)


def _win_attn(aqk, proj, av_blk, sink, batch, seq, ctx_len, with_ctx):
    nl = batch * seq
    nb = seq // WBLK
    ncb = ctx_len // WBLK
    steps = nb + (ncb if with_ctx else 0)
    rows = nl + (batch * ctx_len if with_ctx else 0)
    qw = A_HEADS * LANE
    kw = A_KV_HEADS * LANE
    k_blk = qw // kw

    def qrow(b, n):
        return jnp.where(n < nb, b * nb + n, nl // WBLK + b * ncb + (n - nb))

    def krow(d):
        return lambda b, n: (b * nb + jnp.clip(n + d, 0, nb - 1), k_blk)

    def vrow(d):
        return lambda b, n: (b * nb + jnp.clip(n + d, 0, nb - 1), av_blk)

    return pl.pallas_call(
        functools.partial(_win_attn_kernel, nb=nb, seq=seq),
        grid=(batch, steps),
        in_specs=[pl.BlockSpec(memory_space=pltpu.SMEM),
                  pl.BlockSpec((WBLK, qw), lambda b, n: (qrow(b, n), 0)),
                  pl.BlockSpec((WBLK, kw), krow(-1)), pl.BlockSpec((WBLK, kw), krow(0)),
                  pl.BlockSpec((WBLK, kw), krow(1)),
                  pl.BlockSpec((WBLK, kw), vrow(-1)), pl.BlockSpec((WBLK, kw), vrow(0)),
                  pl.BlockSpec((WBLK, kw), vrow(1)),
                  pl.BlockSpec((ctx_len, kw), lambda b, n: (nl // ctx_len + b, k_blk)),
                  pl.BlockSpec((ctx_len, kw), lambda b, n: (nl // ctx_len + b, av_blk))],
        out_specs=pl.BlockSpec((WBLK, qw), lambda b, n: (qrow(b, n), 0)),
        out_shape=jax.ShapeDtypeStruct((rows, qw), bf16),
        compiler_params=_cparams(("arbitrary", "arbitrary")),
        name="window_attention",
    )(sink, aqk, aqk, aqk, aqk, proj, proj, proj, aqk, proj)


def _mla_attn_kernel(q_ref, kl_ref, vl_ref, kx_ref, vx_ref, o_ref, *, nq):
    t = pl.program_id(2)
    lat_mask = jnp.broadcast_to(t < nq, (q_ref.shape[0], kl_ref.shape[0]))
    o = _softmax_attend(q_ref[...], [(kl_ref[...], vl_ref[...], lat_mask), (kx_ref[...], vx_ref[...], None)])
    o_ref[...] = o.astype(bf16)


def _mla_attn(cq, ck, kvp, batch, seq, ctx_len, tq, with_ctx):
    nl = batch * seq
    nq = seq // tq
    ncq = ctx_len // tq
    steps = nq + (ncq if with_ctx else 0)
    rows = nl + (batch * ctx_len if with_ctx else 0)

    def qrow(b, t):
        return jnp.where(t < nq, b * nq + t, nl // tq + b * ncq + (t - nq))

    return pl.pallas_call(
        functools.partial(_mla_attn_kernel, nq=nq),
        grid=(batch, C_HEADS, steps),
        in_specs=[pl.BlockSpec((tq, 2 * LANE), lambda b, h, t: (qrow(b, t), h)),
                  pl.BlockSpec((seq, 2 * LANE), lambda b, h, t: (b, h)),
                  pl.BlockSpec((seq, LANE), lambda b, h, t: (b, C_HEADS + h)),
                  pl.BlockSpec((ctx_len, 2 * LANE), lambda b, h, t: (nl // ctx_len + b, h)),
                  pl.BlockSpec((ctx_len, LANE), lambda b, h, t: (nl // ctx_len + b, C_HEADS + h))],
        out_specs=pl.BlockSpec((tq, LANE), lambda b, h, t: (qrow(b, t), h)),
        out_shape=jax.ShapeDtypeStruct((rows, C_HEADS * LANE), bf16),
        compiler_params=_cparams(("arbitrary", "arbitrary", "arbitrary")),
        name="mla_attention",
    )(cq, ck, kvp, ck, kvp)


HALO = 16


def _gdn_prep_kernel(x_ref, xp_ref, xn_ref, w_ref, pab_ref, alog_ref, dtb_ref, o_ref, gb_ref, scr, *,
                     tm, lat_tiles, seq_tiles, ctx_tiles):
    i = pl.program_id(0)
    p = pl.program_id(1)
    is_lat = i < lat_tiles
    pos = jnp.where(is_lat, i % seq_tiles, (i - lat_tiles) % ctx_tiles)
    last = jnp.where(is_lat, seq_tiles - 1, ctx_tiles - 1)
    has_prev = (pos != 0).astype(f32)
    has_next = (pos != last).astype(f32)
    scr[0:8, :] = xp_ref[...].astype(f32)[HALO - 8:HALO, :] * has_prev
    scr[8:8 + tm, :] = x_ref[...].astype(f32)
    scr[8 + tm:16 + tm, :] = xn_ref[...].astype(f32)[0:8, :] * has_next
    w = w_ref[...]
    acc = None
    for k in range(CONV_K):
        term = scr[pl.ds(8 - CONV_K // 2 + k, tm), :] * w[k:k + 1, :]
        acc = term if acc is None else acc + term
    u = acc * jax.nn.sigmoid(acc)
    scale = jnp.where(p == 0, B_DK ** -0.5, 1.0)
    for h in range(B_HEADS):
        sl = slice(h * LANE, (h + 1) * LANE)
        uh = u[:, sl]
        nrm = uh * lax.rsqrt(jnp.sum(uh * uh, axis=-1, keepdims=True) + NORM_EPS) * scale
        o_ref[:, sl] = jnp.where(p == 2, uh, nrm).astype(bf16)

    @pl.when(p == 0)
    def _():
        pab = pab_ref[...]
        lane = lax.broadcasted_iota(i32, pab.shape, 1)
        t = pab + dtb_ref[...]
        softplus = jnp.maximum(t, 0.0) + jnp.log(1.0 + jnp.exp(-jnp.abs(t)))
        g = -jnp.exp(alog_ref[...]) * softplus
        beta = jax.nn.sigmoid(pab)
        gb_ref[...] = jnp.where(lane < 2 * B_HEADS, g, jnp.where(lane < 4 * B_HEADS, beta, 0.0))


def _gdn_prep(proj, pab, w_conv, a_log, dt_bias, st):
    rows, tm = st["rows"], st["tm_seq"]
    d3 = 3 * st["d"]
    w = B_HEADS * LANE
    col0 = (d3 + REL_BQKV) // w
    assert (d3 + REL_BQKV) % w == 0 and tm % HALO == 0
    pad = lambda a: jnp.concatenate([a.reshape(1, -1).astype(f32), jnp.zeros((1, LANE - 2 * B_HEADS), f32)], axis=1)
    hb = tm // HALO
    n_halo = rows // HALO
    kern = functools.partial(_gdn_prep_kernel, tm=tm, lat_tiles=st["nl"] // tm, seq_tiles=st["seq"] // tm,
                             ctx_tiles=st["ctx_len"] // tm)
    return pl.pallas_call(
        kern,
        grid=(rows // tm, 3),
        in_specs=[pl.BlockSpec((tm, w), lambda i, p: (i, col0 + p)),
                  pl.BlockSpec((HALO, w), lambda i, p: (jnp.maximum(i * hb - 1, 0), col0 + p)),
                  pl.BlockSpec((HALO, w), lambda i, p: (jnp.minimum((i + 1) * hb, n_halo - 1), col0 + p)),
                  pl.BlockSpec((CONV_K, w), lambda i, p: (0, p)),
                  pl.BlockSpec((tm, LANE), lambda i, p: (i, 0)),
                  pl.BlockSpec((1, LANE), lambda i, p: (0, 0)),
                  pl.BlockSpec((1, LANE), lambda i, p: (0, 0))],
        out_specs=[pl.BlockSpec((tm, w), lambda i, p: (i, p)), pl.BlockSpec((tm, LANE), lambda i, p: (i, 0))],
        out_shape=[jax.ShapeDtypeStruct((rows, 3 * w), bf16), jax.ShapeDtypeStruct((rows, LANE), f32)],
        scratch_shapes=[pltpu.VMEM((tm + 16, w), f32)],
        compiler_params=_cparams(("arbitrary", "arbitrary")),
        name="gdn_prep",
    )(proj, proj, proj, w_conv, pab, pad(a_log), pad(dt_bias))


def _gdn_scan_kernel(qf_ref, kf_ref, vf_ref, gf_ref, qb_ref, kb_ref, vb_ref, gb_ref, of_ref, ob_ref, sf_scr, sb_scr):
    c = GDN_CHUNK

    @pl.when(pl.program_id(1) == 0)
    def _():
        sf_scr[...] = jnp.zeros_like(sf_scr)
        sb_scr[...] = jnp.zeros_like(sb_scr)

    ri = lax.broadcasted_iota(i32, (c, c), 0)
    ci = lax.broadcasted_iota(i32, (c, c), 1)
    lane = lax.broadcasted_iota(i32, (c, LANE), 1)
    eye = (ri == ci).astype(f32)
    dirs = ((qf_ref, kf_ref, vf_ref, gf_ref, of_ref, sf_scr), (qb_ref, kb_ref, vb_ref, gb_ref, ob_ref, sb_scr))
    chains = []
    for direction, (q_ref, k_ref, v_ref, g_ref, o_ref, s_scr) in enumerate(dirs):
        incl = (ri >= ci) if direction == 0 else (ri <= ci)
        strict = (ri > ci) if direction == 0 else (ri < ci)
        tri = incl.astype(bf16)
        gbv = g_ref[...]
        hi, mid, lo = _split3(gbv)
        dec_all = _dot(tri, hi) + _dot(tri, mid) + _dot(tri, lo)
        tot_all = dec_all[c - 1:c, :] if direction == 0 else dec_all[0:1, :]
        for h in range(B_HEADS):
            lg = direction * B_HEADS + h
            lb = 2 * B_HEADS + lg
            chains.append(dict(h=h, q_ref=q_ref, k_ref=k_ref, v_ref=v_ref, o_ref=o_ref, s_scr=s_scr, incl=incl,
                               strict=strict, dec=dec_all[:, lg:lg + 1], beta=gbv[:, lb:lb + 1],
                               tot=tot_all[:, lg:lg + 1], sl=slice(h * LANE, (h + 1) * LANE)))
    for ch in chains:
        dh, dm, dl = (part.astype(f32) for part in _split3(ch["dec"]))
        lhs = jnp.where(lane == 0, dh, jnp.where(lane == 1, dm, jnp.where(lane == 2, dl,
                                                                          jnp.where(lane < 6, 1.0, 0.0))))
        rhs = jnp.where(lane < 3, 1.0, jnp.where(lane == 3, -dh, jnp.where(lane == 4, -dm,
                                                                           jnp.where(lane == 5, -dl, 0.0))))
        ch["diff"] = _dot_nt(lhs.astype(bf16), rhs.astype(bf16))
    for ch in chains:
        k_b = ch["k_ref"][:, ch["sl"]]
        q = ch["q_ref"][:, ch["sl"]].astype(f32)
        ch["kbeta"] = k_b.astype(f32) * ch["beta"]
        ch["kk"] = _dot_nt(jnp.concatenate([ch["kbeta"], q], axis=0).astype(bf16), k_b)
    for ch in chains:
        gamma = jnp.exp(jnp.where(ch["incl"], ch.pop("diff"), NEG))
        kk = ch.pop("kk")
        a_str = jnp.where(ch["strict"], kk[:c] * gamma, 0.0)
        ch["a_qk"] = (kk[c:] * gamma).astype(bf16)
        ch["x"] = eye - a_str
        a_b = a_str.astype(bf16)
        ch["pw"] = _dot(a_b, a_b)
    for _ in range(int(math.log2(c)) - 2):
        for ch in chains:
            xp = _dot(jnp.concatenate([ch["x"], ch["pw"]], axis=0).astype(bf16), ch["pw"].astype(bf16))
            ch["x"] = ch["x"] + xp[:c]
            ch["pw"] = xp[c:]
    for ch in chains:
        ch["x"] = ch["x"] + _dot(ch["x"].astype(bf16), ch.pop("pw").astype(bf16))
    for ch in chains:
        edec = jnp.exp(ch["dec"])
        v = ch["v_ref"][:, ch["sl"]].astype(f32)
        rhs = jnp.concatenate([v * ch["beta"], ch.pop("kbeta") * edec], axis=1).astype(bf16)
        ch["sol"] = _dot(ch.pop("x").astype(bf16), rhs)
    for ch in chains:
        sol = ch.pop("sol")
        q = ch["q_ref"][:, ch["sl"]].astype(f32)
        ch["state"] = ch["s_scr"][ch["h"]]
        ch["u"] = sol[:, :B_DV]
        lhs = jnp.concatenate([sol[:, B_DV:], q * jnp.exp(ch["dec"])], axis=0).astype(bf16)
        ch["ws"] = _dot(lhs, ch["state"].astype(bf16))
    for ch in chains:
        ws = ch.pop("ws")
        v_new = (ch.pop("u") - ws[:c]).astype(bf16)
        k = ch["k_ref"][:, ch["sl"]].astype(f32)
        kd = (k * jnp.exp(ch["tot"] - ch["dec"])).astype(bf16)
        ch["o_ref"][:, ch["sl"]] = ws[c:] + _dot(ch.pop("a_qk"), v_new)
        ch["s_scr"][ch["h"]] = ch.pop("state") * jnp.exp(ch["tot"]) + _dot_tn(kd, v_new)


def _gdn_scan(qkvn, gb, st):
    batch, nl, rows = st["batch"], st["nl"], st["rows"]
    c = GDN_CHUNK
    ncc, nlc = st["ctx_len"] // c, st["seq"] // c
    w = B_HEADS * LANE

    def fwd(b, s):
        return jnp.where(s < ncc, nl // c + b * ncc + s, b * nlc + (s - ncc))

    def bwd(b, s):
        return jnp.where(s < ncc, nl // c + b * ncc + (ncc - 1 - s), b * nlc + (nlc - 1 - (s - ncc)))

    def specs(rowf):
        return [pl.BlockSpec((c, w), lambda b, s, j=j: (rowf(b, s), j)) for j in range(3)] + \
               [pl.BlockSpec((c, LANE), lambda b, s: (rowf(b, s), 0))]

    return pl.pallas_call(
        _gdn_scan_kernel,
        grid=(batch, ncc + nlc),
        in_specs=specs(fwd) + specs(bwd),
        out_specs=[pl.BlockSpec((c, w), lambda b, s: (fwd(b, s), 0)), pl.BlockSpec((c, w), lambda b, s: (bwd(b, s), 0))],
        out_shape=[jax.ShapeDtypeStruct((rows, w), f32), jax.ShapeDtypeStruct((rows, w), f32)],
        scratch_shapes=[pltpu.VMEM((B_HEADS, B_DK, B_DV), f32), pltpu.VMEM((B_HEADS, B_DK, B_DV), f32)],
        compiler_params=_cparams(("arbitrary", "arbitrary")),
        name="gdn_scan",
    )(qkvn, qkvn, qkvn, gb, qkvn, qkvn, qkvn, gb)


def _gdn_out_kernel(of_ref, ob_ref, z_ref, gain_ref, y_ref):
    o = of_ref[...] + ob_ref[...]
    gain = gain_ref[...]
    for h in range(B_HEADS):
        sl = slice(h * LANE, (h + 1) * LANE)
        oh = o[:, sl]
        y = oh * lax.rsqrt(jnp.mean(oh * oh, axis=-1, keepdims=True) + NORM_EPS) * gain
        z = z_ref[:, sl].astype(f32)
        y_ref[:, sl] = (y * (z * jax.nn.sigmoid(z))).astype(bf16)


def _gdn_out(o_f, o_b, proj, gain, st):
    rows, tm = st["rows"], st["tm_big"]
    w = B_HEADS * LANE
    z_blk = (3 * st["d"] + REL_BZ) // w
    spec = pl.BlockSpec((tm, w), lambda i: (i, 0))
    return pl.pallas_call(
        _gdn_out_kernel,
        grid=(rows // tm,),
        in_specs=[spec, spec, pl.BlockSpec((tm, w), lambda i: (i, z_blk)), pl.BlockSpec((1, B_DV), lambda i: (0, 0))],
        out_specs=spec,
        out_shape=jax.ShapeDtypeStruct((rows, w), bf16),
        compiler_params=_cparams(("arbitrary",)),
        name="gdn_out",
    )(o_f, o_b, proj, gain.reshape(1, B_DV))


def _mixer_b(proj, pab, w_conv, a_log, dt_bias, out_gain, st):
    qkvn, gb = _gdn_prep(proj, pab, w_conv, a_log, dt_bias, st)
    o_f, o_b = _gdn_scan(qkvn, gb, st)
    return _gdn_out(o_f, o_b, proj, out_gain, st)


def _merge_kernel(ya_ref, yb_ref, yc_ref, wa_ref, wb_ref, wc_ref, ga_ref, gb_ref, gc_ref, o_ref):
    acc = None
    for y_ref, w_ref, g_ref in ((ya_ref, wa_ref, ga_ref), (yb_ref, wb_ref, gb_ref), (yc_ref, wc_ref, gc_ref)):
        t = jax.nn.sigmoid(g_ref[...].astype(f32)) * _dot(y_ref[...], w_ref[...])
        acc = t if acc is None else acc + t
    o_ref[...] = acc.astype(bf16)


def _merge(ya, yb, yc, w_branch_b, proj, st, rows):
    d, tm = st["d"], st["tm_big"]
    tn = _pick((1024, 512, 256, 128), d)
    bw = ya.shape[1]
    nj = d // tn
    y_spec = pl.BlockSpec((tm, bw), lambda i, j: (i, 0))
    w_specs = [pl.BlockSpec((bw, tn), lambda i, j, r=r: (r, j)) for r in range(3)]
    g_specs = [pl.BlockSpec((tm, tn), lambda i, j, r=r: (i, r * nj + j)) for r in range(3)]
    return pl.pallas_call(
        _merge_kernel,
        grid=(rows // tm, nj),
        in_specs=[y_spec, y_spec, y_spec] + w_specs + g_specs,
        out_specs=pl.BlockSpec((tm, tn), lambda i, j: (i, j)),
        out_shape=jax.ShapeDtypeStruct((rows, d), bf16),
        compiler_params=_cparams(("arbitrary", "arbitrary")),
        name="branch_merge",
    )(ya, yb, yc, w_branch_b, w_branch_b, w_branch_b, proj, proj, proj)


def _out_proj_kernel(m_ref, w_ref, x_ref, g_ref, o_ref):
    o_ref[...] = x_ref[...] + g_ref[0] * _dot(m_ref[...], w_ref[...])


def _out_proj(mixed, w_o_b, xall, gate, st, rows):
    d, tm = st["d"], st["tm_big"]
    tn = _pick((1024, 512, 256, 128), d)
    group = _group_of_tile(st, tm)
    return pl.pallas_call(
        _out_proj_kernel,
        grid=(rows // tm, d // tn),
        in_specs=[pl.BlockSpec((tm, d), lambda i, j: (i, 0)),
                  pl.BlockSpec((d, tn), lambda i, j: (0, j)),
                  pl.BlockSpec((tm, tn), lambda i, j: (i, j)),
                  pl.BlockSpec((1, 1, tn), lambda i, j: (group(i), 0, j))],
        out_specs=pl.BlockSpec((tm, tn), lambda i, j: (i, j)),
        out_shape=jax.ShapeDtypeStruct((rows, d), f32),
        compiler_params=_cparams(("arbitrary", "arbitrary")),
        name="out_proj",
    )(mixed, w_o_b, xall, gate)


def _norm_router_kernel(x_ref, s_ref, sh_ref, wr_ref, br_ref, h_ref, aff_ref):
    x = x_ref[...]
    h = x * lax.rsqrt(jnp.mean(x * x, axis=-1, keepdims=True) + NORM_EPS) * s_ref[0] + sh_ref[0]
    h_ref[...] = h
    logits = _dot_nt(wr_ref[...], h.astype(bf16)) + br_ref[...]
    e = jnp.exp(logits - jnp.max(logits, axis=0, keepdims=True))
    aff_ref[...] = e / jnp.sum(e, axis=0, keepdims=True)


def _norm_router(x, lay, w_router, b_router, st, rows):
    d, tm = st["d"], st["tm_big"]
    group = _group_of_tile(st, tm)
    ne = w_router.shape[1]
    return pl.pallas_call(
        _norm_router_kernel,
        grid=(rows // tm,),
        in_specs=[pl.BlockSpec((tm, d), lambda i: (i, 0)),
                  pl.BlockSpec((1, 1, d), lambda i: (group(i), 0, 0)),
                  pl.BlockSpec((1, 1, d), lambda i: (group(i), 0, 0)),
                  pl.BlockSpec((ne, d), lambda i: (0, 0)),
                  pl.BlockSpec((ne, 1), lambda i: (0, 0))],
        out_specs=[pl.BlockSpec((tm, d), lambda i: (i, 0)), pl.BlockSpec((ne, tm), lambda i: (0, i))],
        out_shape=[jax.ShapeDtypeStruct((rows, d), f32), jax.ShapeDtypeStruct((ne, rows), f32)],
        compiler_params=_cparams(("arbitrary",)),
        name="norm_router",
    )(x, lay["scale2"], lay["shift2"], w_router.T.astype(bf16), b_router.reshape(ne, 1).astype(f32))


PREFIX_W = 256


def _prefix_excl(mask):
    ne, t = mask.shape
    w = min(PREFIX_W, t)
    r = lax.broadcasted_iota(i32, (w, w), 0)
    c = lax.broadcasted_iota(i32, (w, w), 1)
    upper = jnp.where(r < c, 1.0, 0.0).astype(bf16)
    off = jnp.zeros((ne, 1), f32)
    outs = []
    for j in range(t // w):
        xb = jnp.where(mask[:, j * w:(j + 1) * w], 1.0, 0.0)
        outs.append(_dot(xb.astype(bf16), upper) + off)
        off = off + jnp.sum(xb, axis=1, keepdims=True)
    return jnp.concatenate(outs, axis=1) if len(outs) > 1 else outs[0]


def _topk_kernel(aff_ref, idx_ref, gate_ref, *, t, cap, base0):
    g = pl.program_id(0)
    aff = aff_ref[...]
    ne = aff.shape[0]
    bits = pltpu.bitcast(aff, i32)
    thr = jnp.zeros((ne, 1), i32)
    for bit in range(30, -1, -1):
        cand = thr | (1 << bit)
        cnt = jnp.sum(jnp.where(bits >= cand, 1.0, 0.0), axis=1, keepdims=True)
        thr = jnp.where(cnt >= cap, cand, thr)
    gt = bits > thr
    eq = bits == thr
    need = cap - jnp.sum(jnp.where(gt, 1.0, 0.0), axis=1, keepdims=True)
    sel = gt | (eq & (_prefix_excl(eq) < need))
    pos = _prefix_excl(sel).astype(i32)
    tok = lax.broadcasted_iota(i32, (1, t), 1)
    t_hi = (tok >> 8).astype(f32)
    t_lo = (tok & 255).astype(f32)
    slot = lax.broadcasted_iota(i32, (cap, t), 0)
    base = base0 + g * t
    for e in range(ne):
        onehot = jnp.where((slot == pos[e:e + 1, :]) & sel[e:e + 1, :], 1.0, 0.0).astype(bf16)
        g_hi, g_mid, g_lo = _split3(aff[e:e + 1, :])
        vals = jnp.concatenate([t_hi.astype(bf16), t_lo.astype(bf16), g_hi, g_mid, g_lo,
                                jnp.zeros((3, t), bf16)], axis=0)
        out = _dot_nt(vals, onehot)
        idx_ref[0, e:e + 1, :] = base + (out[0:1] * 256.0 + out[1:2]).astype(i32)
        gate_ref[0, e:e + 1, :] = out[2:3] + out[3:4] + out[4:5]


def _topk(aff_t, n_groups, t, cap, col_blk0, base0):
    ne = aff_t.shape[0]
    return pl.pallas_call(
        functools.partial(_topk_kernel, t=t, cap=cap, base0=base0),
        grid=(n_groups,),
        in_specs=[pl.BlockSpec((ne, t), lambda g: (0, col_blk0 + g))],
        out_specs=[pl.BlockSpec((1, ne, cap), lambda g: (g, 0, 0)), pl.BlockSpec((1, ne, cap), lambda g: (g, 0, 0))],
        out_shape=[jax.ShapeDtypeStruct((n_groups, ne, cap), i32), jax.ShapeDtypeStruct((n_groups, ne, cap), f32)],
        compiler_params=_cparams(("arbitrary",)),
        name="expert_topk",
    )(aff_t)


BF16_ROWS = 16


def _pieces(m, want):
    for n in range(want, 0, -1):
        if m % (n * BF16_ROWS) == 0:
            return n
    raise ValueError(m)


def _ffn1_kernel(idx_ref, h_hbm, wg_ref, wu_ref, o_ref, stage, xs, sem, *, m):
    chunk = m // _pieces(m, 2)

    @pl.when(pl.program_id(1) == 0)
    def _():
        for c0 in range(0, m, chunk):
            def start(r, carry, c0=c0):
                row = idx_ref[0, 0, c0 + r]
                pltpu.make_async_copy(h_hbm.at[pl.ds(row, 1), :], stage.at[pl.ds(r, 1), :], sem).start()
                return carry

            def wait(r, carry):
                pltpu.make_async_copy(h_hbm.at[pl.ds(0, 1), :], stage.at[pl.ds(r, 1), :], sem).wait()
                return carry

            lax.fori_loop(0, chunk, start, 0)
            lax.fori_loop(0, chunk, wait, 0)
            xs[c0:c0 + chunk, :] = stage[...].astype(bf16)

    wg = wg_ref[0].astype(bf16)
    wu = wu_ref[0].astype(bf16)
    rc = m // _pieces(m, 4)
    for r0 in range(0, m, rc):
        xr = xs[r0:r0 + rc, :]
        a = _dot(xr, wg)
        o_ref[0, r0:r0 + rc, :] = (a * jax.nn.sigmoid(a) * _dot(xr, wu)).astype(bf16)


def _ffn1(idx, h2, w_gate, w_up, layer):
    ne, m = idx.shape
    d, ff = w_gate.shape[2], w_gate.shape[3]
    tf = _pick((512, 256, 128), ff)
    w_gate = w_gate.reshape(-1, d, ff)
    w_up = w_up.reshape(-1, d, ff)
    return pl.pallas_call(
        functools.partial(_ffn1_kernel, m=m),
        grid=(ne, ff // tf),
        in_specs=[pl.BlockSpec((1, 1, m), lambda e, f: (e, 0, 0), memory_space=pltpu.SMEM),
                  pl.BlockSpec(memory_space=pl.ANY),
                  pl.BlockSpec((1, d, tf), lambda e, f: (layer * ne + e, 0, f)),
                  pl.BlockSpec((1, d, tf), lambda e, f: (layer * ne + e, 0, f))],
        out_specs=pl.BlockSpec((1, m, tf), lambda e, f: (e, 0, f)),
        out_shape=jax.ShapeDtypeStruct((ne, m, ff), bf16),
        scratch_shapes=[pltpu.VMEM((m // _pieces(m, 2), d), f32), pltpu.VMEM((m, d), bf16),
                        pltpu.SemaphoreType.DMA(())],
        compiler_params=_cparams(("arbitrary", "arbitrary")),
        name="expert_ffn_in",
    )(idx.reshape(ne, 1, m), h2, w_gate, w_up)


def _ffn2_kernel(h_ref, w_ref, g_ref, o_ref):
    o_ref[0] = _dot(h_ref[0], w_ref[0].astype(bf16)) * g_ref[0]


def _ffn2(hid, w_down, gates, layer):
    ne, m, ff = hid.shape
    d = w_down.shape[3]
    td = _pick((512, 256, 128), d)
    w_down = w_down.reshape(-1, ff, d)
    return pl.pallas_call(
        _ffn2_kernel,
        grid=(ne, d // td),
        in_specs=[pl.BlockSpec((1, m, ff), lambda e, j: (e, 0, 0)),
                  pl.BlockSpec((1, ff, td), lambda e, j: (layer * ne + e, 0, j)),
                  pl.BlockSpec((1, m, 1), lambda e, j: (e, 0, 0))],
        out_specs=pl.BlockSpec((1, m, td), lambda e, j: (e, 0, j)),
        out_shape=jax.ShapeDtypeStruct((ne, m, d), f32),
        compiler_params=_cparams(("arbitrary", "arbitrary")),
        name="expert_ffn_out",
    )(hid, w_down, gates.reshape(ne, m, 1))


def _combine_kernel(*refs, cap, aliased):
    idx_ref, y_ref = refs[0], refs[1]
    o_ref = refs[-1]

    @pl.when(pl.program_id(2) == 0)
    def _():
        o_ref[...] = jnp.zeros_like(o_ref)

    def body(c, carry):
        r = idx_ref[0, 0, c]
        o_ref[pl.ds(r, 1), :] = o_ref[pl.ds(r, 1), :] + y_ref[0, pl.ds(c, 1), :]
        return carry

    lax.fori_loop(0, cap, body, 0, unroll=8)


def _combine(local_idx, y, n_groups, t, cap, y_blk0, out_blk0, out_rows, prev=None):
    ne, _, d = y.shape
    td = _pick((1024, 512, 256, 128), d)
    in_specs = [pl.BlockSpec((1, 1, cap), lambda g, j, e: (e * n_groups + g, 0, 0), memory_space=pltpu.SMEM),
                pl.BlockSpec((1, cap, td), lambda g, j, e: (e, y_blk0 + g, j))]
    args = [local_idx.reshape(ne * n_groups, 1, cap), y]
    aliases = {}
    if prev is not None:
        in_specs.append(pl.BlockSpec(memory_space=pl.ANY))
        args.append(prev)
        aliases = {2: 0}
    return pl.pallas_call(
        functools.partial(_combine_kernel, cap=cap, aliased=prev is not None),
        grid=(n_groups, d // td, ne),
        in_specs=in_specs,
        out_specs=pl.BlockSpec((t, td), lambda g, j, e: (out_blk0 + g, j)),
        out_shape=jax.ShapeDtypeStruct((out_rows, d), f32),
        input_output_aliases=aliases,
        compiler_params=_cparams(("arbitrary", "arbitrary", "arbitrary")),
        name="expert_combine",
    )(*args)


def _moe(x_mid, lay, w_router, b_router, w_gate, w_up, w_down, layer, st, with_ctx):
    batch, seq, ctx_len, nl = st["batch"], st["seq"], st["ctx_len"], st["nl"]
    ne = w_router.shape[1]
    rows = nl + (st["nc"] if with_ctx else 0)
    h2, aff_t = _norm_router(x_mid, lay, w_router, b_router, st, rows)
    cap = EC_CAPACITY * seq // ne
    idx_l, gate_l = _topk(aff_t, batch, seq, cap, 0, 0)
    idx = jnp.swapaxes(idx_l, 0, 1).reshape(ne, batch * cap)
    gates = jnp.swapaxes(gate_l, 0, 1).reshape(ne, batch * cap)
    loc_l = jnp.swapaxes(idx_l - (jnp.arange(batch, dtype=i32) * seq)[:, None, None], 0, 1)
    if with_ctx:
        cap_c = EC_CAPACITY * ctx_len // ne
        idx_c, gate_c = _topk(aff_t, batch, ctx_len, cap_c, nl // ctx_len, nl)
        idx = jnp.concatenate([idx, jnp.swapaxes(idx_c, 0, 1).reshape(ne, batch * cap_c)], axis=1)
        gates = jnp.concatenate([gates, jnp.swapaxes(gate_c, 0, 1).reshape(ne, batch * cap_c)], axis=1)
        loc_c = jnp.swapaxes(idx_c - (nl + jnp.arange(batch, dtype=i32) * ctx_len)[:, None, None], 0, 1)
    hid = _ffn1(idx, h2, w_gate, w_up, layer)
    y = _ffn2(hid, w_down, gates, layer)
    out = _combine(loc_l, y, batch, seq, cap, 0, 0, rows)
    if with_ctx:
        out = _combine(loc_c, y, batch, ctx_len, cap_c, batch * cap // cap_c, nl // ctx_len, rows, prev=out)
    return out


def _pack_w_in(w):
    d = w.shape[0]
    sizes = (A_HEADS * HEAD_DIM, A_KV_HEADS * HEAD_DIM, A_KV_HEADS * HEAD_DIM,
             B_HEADS * (2 * B_DK + B_DV), B_HEADS * B_DV, 2 * B_HEADS, 2 * B_HEADS,
             C_Q_RANK, C_KV_RANK, C_ROPE, 3 * d)
    aq, ak, av, bqkv, bz, ba, bb, cq, ckv, ckr, gt = jnp.split(w, np.cumsum(sizes)[:-1].tolist(), axis=1)
    z = lambda n: jnp.zeros((d, n), w.dtype)
    packed = jnp.concatenate(
        [gt, cq, ckv, ckr, z(LANE - C_ROPE), ba, bb, z(LANE - 4 * B_HEADS), aq, ak, av, bqkv, bz], axis=1)
    assert packed.shape[1] == 3 * d + REL_END
    return packed.astype(bf16)


def _pack_cq_up(w):
    r = w.shape[0]
    w3 = w.reshape(r, C_HEADS, C_NOPE + C_ROPE)
    w3 = jnp.concatenate([w3, jnp.zeros((r, C_HEADS, 2 * LANE - C_NOPE - C_ROPE), w.dtype)], axis=2)
    return w3.reshape(r, C_HEADS * 2 * LANE).astype(bf16)


def _pack_ckv_up(w):
    r = w.shape[0]
    w3 = w.reshape(r, C_HEADS, C_NOPE + C_V)
    return jnp.concatenate([w3[:, :, :C_NOPE].reshape(r, -1), w3[:, :, C_NOPE:].reshape(r, -1)], axis=1).astype(bf16)


def _prepare_statics(batch, seq, ctx_len, d):
    nl, nc = batch * seq, batch * ctx_len
    assert seq % GRID_W == 0 and seq % WBLK == 0 and ctx_len % WBLK == 0 and nl % ctx_len == 0
    st = dict(batch=batch, seq=seq, ctx_len=ctx_len, d=d, nl=nl, nc=nc, rows=nl + nc)
    st["tm_big"] = _pick((1024, 512, 256, 128), seq, nc)
    st["tm_proj"] = _pick((512, 256, 128), seq, nc)
    st["tm_seq"] = _pick((256, 128, 64), seq, ctx_len)
    st["tq"] = _pick((256, 128), seq, ctx_len)
    st["tabs_a"] = _rope_tables(batch, seq, nc, HEAD_DIM)
    st["tabs_c"] = _rope_tables(batch, seq, nc, C_ROPE)
    return st


def _group_of_tile(st, tm):
    per_seq = st["seq"] // tm
    batch = st["batch"]
    return lambda i: jnp.minimum(i // per_seq, batch)


def _layer_mods(mod_l, batch, g1, g2):
    sh1, sc1, gt1, sh2, sc2, gt2 = jnp.split(mod_l[:batch + 1], 6, axis=-1)
    r3 = lambda a: a[:, None, :]
    return dict(scale1=r3(g1[None, :] * (1.0 + sc1)), shift1=r3(sh1), gate1=r3(gt1),
                scale2=r3(g2[None, :] * (1.0 + sc2)), shift2=r3(sh2), gate2=r3(gt2))


def _proj_in(xall, lay, w_in_p, st, delta=None, gate=None):
    d = st["d"]
    tm = st["tm_proj"]
    tn = _pick((1280, 768, 640, 384, 256, 128), w_in_p.shape[1])
    return _norm_matmul(xall, 0, d, lay["scale1"], lay["shift1"], w_in_p, _group_of_tile(st, tm), tm, tn,
                        delta=delta, gate=gate, side_col=3 * d + REL_AB, name="proj_in")


def _mixer_a(proj, sink, st, with_ctx):
    d3 = 3 * st["d"]
    aqk = _prep_a(proj, st["tabs_a"], (d3 + REL_AQ) // ((A_HEADS + A_KV_HEADS) * LANE), st["tm_big"])
    return _win_attn(aqk, proj, (d3 + REL_AV) // (A_KV_HEADS * LANE), sink, st["batch"], st["seq"], st["ctx_len"],
                     with_ctx)


def _mixer_c(proj, q_gain, cq_up_p, kv_gain, ckv_up_p, st, with_ctx):
    d3 = 3 * st["d"]
    tm = st["tm_big"]
    one = lambda i: 0
    qp = _norm_matmul(proj, (d3 + REL_CQ) // C_Q_RANK, C_Q_RANK, q_gain.reshape(1, 1, -1),
                      jnp.zeros((1, 1, C_Q_RANK), f32), cq_up_p, one, tm, cq_up_p.shape[1], name="mla_q_up")[0]
    kvp = _norm_matmul(proj, (d3 + REL_CKV) // C_KV_RANK, C_KV_RANK, kv_gain.reshape(1, 1, -1),
                       jnp.zeros((1, 1, C_KV_RANK), f32), ckv_up_p, one, tm, ckv_up_p.shape[1], name="mla_kv_up")[0]
    cq, ck = _prep_c(qp, kvp, proj, st["tabs_c"], (d3 + REL_CKR) // LANE, tm)
    return _mla_attn(cq, ck, kvp, st["batch"], st["seq"], st["ctx_len"], st["tq"], with_ctx)


def kernel(x, c, ctx, c_ctx, w_mod, b_mod, g_norm1, g_norm2, w_in, a_sink, b_conv, b_a_log, b_dt_bias, b_out_gain,
           c_q_gain, c_q_up, c_kv_gain, c_kv_up, w_branch, w_o, w_router, b_router, w_exp_gate, w_exp_up,
           w_exp_down, g_final):
    batch, seq, d = x.shape
    ctx_len = ctx.shape[1]
    depth = w_in.shape[0]
    assert batch + 1 <= 8
    st = _prepare_statics(batch, seq, ctx_len, d)
    nl, rows_all = st["nl"], st["rows"]
    c_all = jnp.concatenate([c, c_ctx[None], jnp.zeros((8 - batch - 1, d), f32)], axis=0)
    mods = _mod(c_all, w_mod, b_mod)
    xall = jnp.concatenate([x.reshape(nl, d), ctx.reshape(batch * ctx_len, d)], axis=0)
    delta = gate_prev = None
    for l in range(depth):
        need_ctx = l < depth - 1
        lay = _layer_mods(mods[l], batch, g_norm1[l], g_norm2[l])
        w_in_p = _pack_w_in(w_in[l])
        if delta is None:
            proj, pab = _proj_in(xall, lay, w_in_p, st)
        else:
            proj, xall, pab = _proj_in(xall, lay, w_in_p, st, delta=delta, gate=gate_prev)
        ya = _mixer_a(proj, a_sink[l], st, need_ctx)
        yb = _mixer_b(proj, pab, b_conv[l], b_a_log[l], b_dt_bias[l], b_out_gain[l], st)
        yc = _mixer_c(proj, c_q_gain[l], _pack_cq_up(c_q_up[l]), c_kv_gain[l], _pack_ckv_up(c_kv_up[l]), st, need_ctx)
        rows = rows_all if need_ctx else nl
        mixed = _merge(ya, yb, yc, w_branch[l].astype(bf16), proj, st, rows)
        xall = _out_proj(mixed, w_o[l].astype(bf16), xall, lay["gate1"], st, rows)
        delta = _moe(xall, lay, w_router[l], b_router[l], w_exp_gate, w_exp_up, w_exp_down, l, st, need_ctx)
        gate_prev = lay["gate2"]
    tm = st["tq"]
    out = _final_norm(xall, delta, gate_prev, g_final, _group_of_tile(st, tm), tm, nl)
    return out.reshape(batch, seq, d)
```

```python
import functools
import math

import numpy as np
import jax
import jax.numpy as jnp
from jax import lax
from jax.experimental import pallas as pl
from jax.experimental.pallas import tpu as pltpu

f32 = jnp.float32
bf16 = jnp.bfloat16
i32 = jnp.int32

GRID_W = 64
HEAD_DIM = 128
ROPE_BASE = 10000.0
NORM_EPS = 1e-6
A_HEADS = 6
A_KV_HEADS = 2
A_GROUP = A_HEADS // A_KV_HEADS
WINDOW = 128
WBLK = 128
B_HEADS = 6
B_DK = 128
B_DV = 128
CONV_K = 5
GDN_CHUNK = 64
C_HEADS = 6
C_Q_RANK = 512
C_KV_RANK = 256
C_NOPE = 128
C_ROPE = 64
C_V = 128
N_EXPERTS = 16
EC_CAPACITY = 2

LANE = 128
NEG = -1e30
VMEM_LIMIT = 56 * 1024 * 1024

COL_GT = 0
REL_CQ = 0
REL_CKV = 512
REL_CKR = 768
REL_AB = 896
REL_AQ = 1024
REL_AK = 1792
REL_AV = 2048
REL_BQKV = 2304
REL_BZ = 4608
REL_END = 5376


def _cparams(sem):
    return pltpu.CompilerParams(dimension_semantics=sem, vmem_limit_bytes=VMEM_LIMIT)


def _pick(cands, *ns):
    for c in cands:
        if all(n % c == 0 for n in ns):
            return c
    raise ValueError(f"no tile in {cands} divides {ns}")


def _dot(a, b):
    return jnp.dot(a, b, preferred_element_type=f32)


def _dot_nt(a, b):
    return lax.dot_general(a, b, (((1,), (1,)), ((), ())), preferred_element_type=f32)


def _dot_tn(a, b):
    return lax.dot_general(a, b, (((0,), (0,)), ((), ())), preferred_element_type=f32)


def _split3(x):
    hi = x.astype(bf16)
    r = x - hi.astype(f32)
    mid = r.astype(bf16)
    lo = (r - mid.astype(f32)).astype(bf16)
    return hi, mid, lo


def _mod_kernel(c_ref, w_ref, b_ref, o_ref):
    c = c_ref[...]
    sc = (c * jax.nn.sigmoid(c)).astype(bf16)
    o_ref[0] = _dot(sc, w_ref[0].astype(bf16)) + b_ref[0]


def _mod(c_all, w_mod, b_mod):
    n_layers, d, n = w_mod.shape
    tn = _pick((1024, 512, 256, 128), n)
    return pl.pallas_call(
        _mod_kernel,
        grid=(n_layers, n // tn),
        in_specs=[
            pl.BlockSpec((8, d), lambda l, j: (0, 0)),
            pl.BlockSpec((1, d, tn), lambda l, j: (l, 0, j)),
            pl.BlockSpec((1, 1, tn), lambda l, j: (l, 0, j)),
        ],
        out_specs=pl.BlockSpec((1, 8, tn), lambda l, j: (l, 0, j)),
        out_shape=jax.ShapeDtypeStruct((n_layers, 8, n), f32),
        compiler_params=_cparams(("arbitrary", "arbitrary")),
        name="adaln_mod",
    )(c_all, w_mod, b_mod.reshape(n_layers, 1, n))


def _norm_matmul_kernel(*refs, has_res, side, eps):
    it = iter(refs)
    x_ref = next(it)
    d_ref = g_ref = None
    if has_res:
        d_ref, g_ref = next(it), next(it)
    s_ref, sh_ref, w_ref, o_ref = next(it), next(it), next(it), next(it)
    xn_ref = next(it) if has_res else None
    side_ref = next(it) if side is not None else None
    h_scr = next(it)
    j = pl.program_id(1)

    @pl.when(j == 0)
    def _():
        x = x_ref[...].astype(f32)
        if has_res:
            x = x + g_ref[0] * d_ref[...].astype(f32)
            xn_ref[...] = x
        ms = jnp.mean(x * x, axis=-1, keepdims=True)
        h = x * lax.rsqrt(ms + eps) * s_ref[0] + sh_ref[0]
        h_scr[...] = h.astype(bf16)

    acc = _dot(h_scr[...], w_ref[...])
    o_ref[...] = acc.astype(o_ref.dtype)
    if side is not None:
        side_j, side_off = side

        @pl.when(j == side_j)
        def _():
            side_ref[...] = acc[:, side_off:side_off + LANE]


def _norm_matmul(x, x_col, k, scale, shift, w, group_of_tile, tm, tn, *, rows=None, delta=None, gate=None,
                 side_col=None, name):
    rows = x.shape[0] if rows is None else rows
    n = w.shape[1]
    assert rows % tm == 0 and n % tn == 0 and w.shape[0] == k
    has_res = delta is not None
    side = None
    if side_col is not None:
        side = (side_col // tn, side_col % tn)
        assert side[1] + LANE <= tn
    in_specs = [pl.BlockSpec((tm, k), lambda i, j: (i, x_col))]
    args = [x]
    if has_res:
        in_specs += [pl.BlockSpec((tm, k), lambda i, j: (i, 0)),
                     pl.BlockSpec((1, 1, k), lambda i, j: (group_of_tile(i), 0, 0))]
        args += [delta, gate]
    in_specs += [pl.BlockSpec((1, 1, k), lambda i, j: (group_of_tile(i), 0, 0)),
                 pl.BlockSpec((1, 1, k), lambda i, j: (group_of_tile(i), 0, 0)),
                 pl.BlockSpec((k, tn), lambda i, j: (0, j))]
    args += [scale, shift, w]
    out_specs = [pl.BlockSpec((tm, tn), lambda i, j: (i, j))]
    out_shape = [jax.ShapeDtypeStruct((rows, n), bf16)]
    if has_res:
        out_specs.append(pl.BlockSpec((tm, k), lambda i, j: (i, 0)))
        out_shape.append(jax.ShapeDtypeStruct((rows, k), f32))
    if side is not None:
        out_specs.append(pl.BlockSpec((tm, LANE), lambda i, j: (i, 0)))
        out_shape.append(jax.ShapeDtypeStruct((rows, LANE), f32))
    return pl.pallas_call(
        functools.partial(_norm_matmul_kernel, has_res=has_res, side=side, eps=NORM_EPS),
        grid=(rows // tm, n // tn),
        in_specs=in_specs,
        out_specs=out_specs,
        out_shape=out_shape,
        scratch_shapes=[pltpu.VMEM((tm, k), bf16)],
        compiler_params=_cparams(("arbitrary", "arbitrary")),
        name=name,
    )(*args)


def _final_norm_kernel(x_ref, d_ref, g_ref, gain_ref, o_ref, *, eps):
    x = x_ref[...] + g_ref[0] * d_ref[...]
    ms = jnp.mean(x * x, axis=-1, keepdims=True)
    o_ref[...] = x * lax.rsqrt(ms + eps) * gain_ref[...]


def _final_norm(x, delta, gate, gain, group_of_tile, tm, rows):
    d = x.shape[1]
    return pl.pallas_call(
        functools.partial(_final_norm_kernel, eps=NORM_EPS),
        grid=(rows // tm,),
        in_specs=[pl.BlockSpec((tm, d), lambda i: (i, 0)),
                  pl.BlockSpec((tm, d), lambda i: (i, 0)),
                  pl.BlockSpec((1, 1, d), lambda i: (group_of_tile(i), 0, 0)),
                  pl.BlockSpec((1, d), lambda i: (0, 0))],
        out_specs=pl.BlockSpec((tm, d), lambda i: (i, 0)),
        out_shape=jax.ShapeDtypeStruct((rows, d), f32),
        compiler_params=_cparams(("arbitrary",)),
        name="final_norm",
    )(x, delta, gate, gain.reshape(1, d))


def _rope_tables(batch, seq, n_ctx_rows, rot):
    half, quarter = rot // 2, rot // 4
    t = np.arange(seq)
    row, col = (t // GRID_W).astype(np.float32), (t % GRID_W).astype(np.float32)
    inv = (ROPE_BASE ** (-np.arange(quarter, dtype=np.float32) / quarter)).astype(np.float32)
    lane = np.arange(LANE)
    pos = np.where((lane % rot) < half, row[:, None], col[:, None]).astype(np.float32)
    ang = pos * inv[lane % quarter][None, :]
    active = (lane < rot)[None, :]
    cos = np.where(active, np.cos(ang), 1.0)
    sin = np.where(active, np.sin(ang), 0.0)
    first = ((lane % half) < quarter)[None, :]
    s1 = np.where(first, -sin, 0.0)
    s2 = np.where(first, 0.0, sin)

    def full(tab, fill):
        lat = np.tile(tab.astype(np.float32), (batch, 1))
        ctx = np.full((n_ctx_rows, LANE), fill, np.float32)
        return jnp.asarray(np.concatenate([lat, ctx], axis=0))

    return full(cos, 1.0), full(s1, 0.0), full(s2, 0.0)


def _rope(x, cos, s1, s2, quarter):
    return x * cos + pltpu.roll(x, LANE - quarter, 1) * s1 + pltpu.roll(x, quarter, 1) * s2


def _prep_a_kernel(p_ref, cos_ref, s1_ref, s2_ref, o_ref, *, scale):
    cos, s1, s2 = cos_ref[...], s1_ref[...], s2_ref[...]
    for h in range(A_HEADS + A_KV_HEADS):
        sl = slice(h * LANE, (h + 1) * LANE)
        y = _rope(p_ref[:, sl].astype(f32), cos, s1, s2, HEAD_DIM // 4)
        if h < A_HEADS:
            y = y * scale
        o_ref[:, sl] = y.astype(bf16)


def _prep_a(proj, tabs, col_blk, tm):
    rows = proj.shape[0]
    w = (A_HEADS + A_KV_HEADS) * LANE
    tab_spec = pl.BlockSpec((tm, LANE), lambda i: (i, 0))
    return pl.pallas_call(
        functools.partial(_prep_a_kernel, scale=HEAD_DIM ** -0.5),
        grid=(rows // tm,),
        in_specs=[pl.BlockSpec((tm, w), lambda i: (i, col_blk)), tab_spec, tab_spec, tab_spec],
        out_specs=pl.BlockSpec((tm, w), lambda i: (i, 0)),
        out_shape=jax.ShapeDtypeStruct((rows, w), bf16),
        compiler_params=_cparams(("arbitrary",)),
        name="rope_a",
    )(proj, *tabs)


def _prep_c_kernel(q_ref, kn_ref, kr_ref, cos_ref, s1_ref, s2_ref, oq_ref, ok_ref, *, scale):
    cos, s1, s2 = cos_ref[...], s1_ref[...], s2_ref[...]
    quarter = C_ROPE // 4
    kr = _rope(kr_ref[...].astype(f32), cos, s1, s2, quarter).astype(bf16)
    for h in range(C_HEADS):
        a, b, c = 2 * h * LANE, (2 * h + 1) * LANE, (2 * h + 2) * LANE
        oq_ref[:, a:b] = (q_ref[:, a:b].astype(f32) * scale).astype(bf16)
        oq_ref[:, b:c] = (_rope(q_ref[:, b:c].astype(f32), cos, s1, s2, quarter) * scale).astype(bf16)
        ok_ref[:, a:b] = kn_ref[:, h * LANE:(h + 1) * LANE]
        ok_ref[:, b:c] = kr


def _prep_c(qp, kvp, proj, tabs, ckr_blk, tm):
    rows = qp.shape[0]
    w = C_HEADS * 2 * LANE
    tab_spec = pl.BlockSpec((tm, LANE), lambda i: (i, 0))
    return pl.pallas_call(
        functools.partial(_prep_c_kernel, scale=(C_NOPE + C_ROPE) ** -0.5),
        grid=(rows // tm,),
        in_specs=[pl.BlockSpec((tm, w), lambda i: (i, 0)),
                  pl.BlockSpec((tm, C_HEADS * LANE), lambda i: (i, 0)),
                  pl.BlockSpec((tm, LANE), lambda i: (i, ckr_blk)),
                  tab_spec, tab_spec, tab_spec],
        out_specs=[pl.BlockSpec((tm, w), lambda i: (i, 0)), pl.BlockSpec((tm, w), lambda i: (i, 0))],
        out_shape=[jax.ShapeDtypeStruct((rows, w), bf16), jax.ShapeDtypeStruct((rows, w), bf16)],
        compiler_params=_cparams(("arbitrary",)),
        name="rope_c",
    )(qp, kvp, proj, *tabs)


def _softmax_attend(q, pieces, sink_col=None):
    scores = []
    for k, _, mask in pieces:
        s = _dot_nt(q, k)
        if mask is not None:
            s = jnp.where(mask, s, NEG)
        scores.append(s)
    m = None
    for s in scores:
        ms = jnp.max(s, axis=-1, keepdims=True)
        m = ms if m is None else jnp.maximum(m, ms)
    if sink_col is not None:
        m = jnp.maximum(m, sink_col)
    l = None
    o = None
    for s, (_, v, _) in zip(scores, pieces):
        p = jnp.exp(s - m)
        ls = jnp.sum(p, axis=-1, keepdims=True)
        os_ = _dot(p.astype(bf16), v)
        l = ls if l is None else l + ls
        o = os_ if o is None else o + os_
    if sink_col is not None:
        l = l + jnp.exp(sink_col - m)
    return o / l


def _win_attn_kernel(sink_ref, q_ref, kp_ref, kc_ref, kn_ref, vp_ref, vc_ref, vn_ref, kx_ref, vx_ref, o_ref, *,
                     nb, seq):
    n = pl.program_id(1)
    is_lat = n < nb
    g3 = A_GROUP * WBLK
    r = lax.broadcasted_iota(i32, (g3, 3 * WBLK), 0)
    c = lax.broadcasted_iota(i32, (g3, 3 * WBLK), 1)
    qpos = n * WBLK + (r % WBLK)
    kpos = (n - 1) * WBLK + c
    band = (jnp.abs(kpos - qpos) <= WINDOW) & (kpos >= 0) & (kpos < seq) & is_lat
    rcol = lax.broadcasted_iota(i32, (g3, 1), 0)
    for hk in range(A_KV_HEADS):
        ks = slice(hk * LANE, (hk + 1) * LANE)
        q = jnp.concatenate(
            [q_ref[:, (hk * A_GROUP + g) * LANE:(hk * A_GROUP + g + 1) * LANE] for g in range(A_GROUP)], axis=0)
        kb = jnp.concatenate([kp_ref[:, ks], kc_ref[:, ks], kn_ref[:, ks]], axis=0)
        vb = jnp.concatenate([vp_ref[:, ks], vc_ref[:, ks], vn_ref[:, ks]], axis=0)
        sink_col = jnp.zeros((g3, 1), f32)
        for g in range(A_GROUP):
            sink_col = jnp.where(rcol // WBLK == g, sink_ref[hk * A_GROUP + g], sink_col)
        o = _softmax_attend(q, [(kb, vb, band), (kx_ref[:, ks], vx_ref[:, ks], None)], sink_col)
        for g in range(A_GROUP):
            h = hk * A_GROUP + g
            o_ref[:, h * LANE:(h + 1) * LANE] = o[g * WBLK:(g + 1) * WBLK].astype(bf16)


def _win_attn(aqk, proj, av_blk, sink, batch, seq, ctx_len, with_ctx):
    nl = batch * seq
    nb = seq // WBLK
    ncb = ctx_len // WBLK
    steps = nb + (ncb if with_ctx else 0)
    rows = nl + (batch * ctx_len if with_ctx else 0)
    qw = A_HEADS * LANE
    kw = A_KV_HEADS * LANE
    k_blk = qw // kw

    def qrow(b, n):
        return jnp.where(n < nb, b * nb + n, nl // WBLK + b * ncb + (n - nb))

    def krow(d):
        return lambda b, n: (b * nb + jnp.clip(n + d, 0, nb - 1), k_blk)

    def vrow(d):
        return lambda b, n: (b * nb + jnp.clip(n + d, 0, nb - 1), av_blk)

    return pl.pallas_call(
        functools.partial(_win_attn_kernel, nb=nb, seq=seq),
        grid=(batch, steps),
        in_specs=[pl.BlockSpec(memory_space=pltpu.SMEM),
                  pl.BlockSpec((WBLK, qw), lambda b, n: (qrow(b, n), 0)),
                  pl.BlockSpec((WBLK, kw), krow(-1)), pl.BlockSpec((WBLK, kw), krow(0)),
                  pl.BlockSpec((WBLK, kw), krow(1)),
                  pl.BlockSpec((WBLK, kw), vrow(-1)), pl.BlockSpec((WBLK, kw), vrow(0)),
                  pl.BlockSpec((WBLK, kw), vrow(1)),
                  pl.BlockSpec((ctx_len, kw), lambda b, n: (nl // ctx_len + b, k_blk)),
                  pl.BlockSpec((ctx_len, kw), lambda b, n: (nl // ctx_len + b, av_blk))],
        out_specs=pl.BlockSpec((WBLK, qw), lambda b, n: (qrow(b, n), 0)),
        out_shape=jax.ShapeDtypeStruct((rows, qw), bf16),
        compiler_params=_cparams(("arbitrary", "arbitrary")),
        name="window_attention",
    )(sink, aqk, aqk, aqk, aqk, proj, proj, proj, aqk, proj)


def _mla_attn_kernel(q_ref, kl_ref, vl_ref, kx_ref, vx_ref, o_ref, *, nq):
    t = pl.program_id(2)
    lat_mask = jnp.broadcast_to(t < nq, (q_ref.shape[0], kl_ref.shape[0]))
    o = _softmax_attend(q_ref[...], [(kl_ref[...], vl_ref[...], lat_mask), (kx_ref[...], vx_ref[...], None)])
    o_ref[...] = o.astype(bf16)


def _mla_attn(cq, ck, kvp, batch, seq, ctx_len, tq, with_ctx):
    nl = batch * seq
    nq = seq // tq
    ncq = ctx_len // tq
    steps = nq + (ncq if with_ctx else 0)
    rows = nl + (batch * ctx_len if with_ctx else 0)

    def qrow(b, t):
        return jnp.where(t < nq, b * nq + t, nl // tq + b * ncq + (t - nq))

    return pl.pallas_call(
        functools.partial(_mla_attn_kernel, nq=nq),
        grid=(batch, C_HEADS, steps),
        in_specs=[pl.BlockSpec((tq, 2 * LANE), lambda b, h, t: (qrow(b, t), h)),
                  pl.BlockSpec((seq, 2 * LANE), lambda b, h, t: (b, h)),
                  pl.BlockSpec((seq, LANE), lambda b, h, t: (b, C_HEADS + h)),
                  pl.BlockSpec((ctx_len, 2 * LANE), lambda b, h, t: (nl // ctx_len + b, h)),
                  pl.BlockSpec((ctx_len, LANE), lambda b, h, t: (nl // ctx_len + b, C_HEADS + h))],
        out_specs=pl.BlockSpec((tq, LANE), lambda b, h, t: (qrow(b, t), h)),
        out_shape=jax.ShapeDtypeStruct((rows, C_HEADS * LANE), bf16),
        compiler_params=_cparams(("arbitrary", "arbitrary", "arbitrary")),
        name="mla_attention",
    )(cq, ck, kvp, ck, kvp)


HALO = 16


def _gdn_prep_kernel(x_ref, xp_ref, xn_ref, w_ref, pab_ref, alog_ref, dtb_ref, o_ref, gb_ref, scr, *,
                     tm, lat_tiles, seq_tiles, ctx_tiles):
    i = pl.program_id(0)
    p = pl.program_id(1)
    is_lat = i < lat_tiles
    pos = jnp.where(is_lat, i % seq_tiles, (i - lat_tiles) % ctx_tiles)
    last = jnp.where(is_lat, seq_tiles - 1, ctx_tiles - 1)
    has_prev = (pos != 0).astype(f32)
    has_next = (pos != last).astype(f32)
    scr[0:8, :] = xp_ref[...].astype(f32)[HALO - 8:HALO, :] * has_prev
    scr[8:8 + tm, :] = x_ref[...].astype(f32)
    scr[8 + tm:16 + tm, :] = xn_ref[...].astype(f32)[0:8, :] * has_next
    w = w_ref[...]
    acc = None
    for k in range(CONV_K):
        term = scr[pl.ds(8 - CONV_K // 2 + k, tm), :] * w[k:k + 1, :]
        acc = term if acc is None else acc + term
    u = acc * jax.nn.sigmoid(acc)
    scale = jnp.where(p == 0, B_DK ** -0.5, 1.0)
    for h in range(B_HEADS):
        sl = slice(h * LANE, (h + 1) * LANE)
        uh = u[:, sl]
        nrm = uh * lax.rsqrt(jnp.sum(uh * uh, axis=-1, keepdims=True) + NORM_EPS) * scale
        o_ref[:, sl] = jnp.where(p == 2, uh, nrm).astype(bf16)

    @pl.when(p == 0)
    def _():
        pab = pab_ref[...]
        lane = lax.broadcasted_iota(i32, pab.shape, 1)
        t = pab + dtb_ref[...]
        softplus = jnp.maximum(t, 0.0) + jnp.log(1.0 + jnp.exp(-jnp.abs(t)))
        g = -jnp.exp(alog_ref[...]) * softplus
        beta = jax.nn.sigmoid(pab)
        gb_ref[...] = jnp.where(lane < 2 * B_HEADS, g, jnp.where(lane < 4 * B_HEADS, beta, 0.0))


def _gdn_prep(proj, pab, w_conv, a_log, dt_bias, st):
    rows, tm = st["rows"], st["tm_seq"]
    d3 = 3 * st["d"]
    w = B_HEADS * LANE
    col0 = (d3 + REL_BQKV) // w
    assert (d3 + REL_BQKV) % w == 0 and tm % HALO == 0
    pad = lambda a: jnp.concatenate([a.reshape(1, -1).astype(f32), jnp.zeros((1, LANE - 2 * B_HEADS), f32)], axis=1)
    hb = tm // HALO
    n_halo = rows // HALO
    kern = functools.partial(_gdn_prep_kernel, tm=tm, lat_tiles=st["nl"] // tm, seq_tiles=st["seq"] // tm,
                             ctx_tiles=st["ctx_len"] // tm)
    return pl.pallas_call(
        kern,
        grid=(rows // tm, 3),
        in_specs=[pl.BlockSpec((tm, w), lambda i, p: (i, col0 + p)),
                  pl.BlockSpec((HALO, w), lambda i, p: (jnp.maximum(i * hb - 1, 0), col0 + p)),
                  pl.BlockSpec((HALO, w), lambda i, p: (jnp.minimum((i + 1) * hb, n_halo - 1), col0 + p)),
                  pl.BlockSpec((CONV_K, w), lambda i, p: (0, p)),
                  pl.BlockSpec((tm, LANE), lambda i, p: (i, 0)),
                  pl.BlockSpec((1, LANE), lambda i, p: (0, 0)),
                  pl.BlockSpec((1, LANE), lambda i, p: (0, 0))],
        out_specs=[pl.BlockSpec((tm, w), lambda i, p: (i, p)), pl.BlockSpec((tm, LANE), lambda i, p: (i, 0))],
        out_shape=[jax.ShapeDtypeStruct((rows, 3 * w), bf16), jax.ShapeDtypeStruct((rows, LANE), f32)],
        scratch_shapes=[pltpu.VMEM((tm + 16, w), f32)],
        compiler_params=_cparams(("arbitrary", "arbitrary")),
        name="gdn_prep",
    )(proj, proj, proj, w_conv, pab, pad(a_log), pad(dt_bias))


def _gdn_scan_kernel(qf_ref, kf_ref, vf_ref, gf_ref, qb_ref, kb_ref, vb_ref, gb_ref, of_ref, ob_ref, sf_scr, sb_scr):
    c = GDN_CHUNK

    @pl.when(pl.program_id(1) == 0)
    def _():
        sf_scr[...] = jnp.zeros_like(sf_scr)
        sb_scr[...] = jnp.zeros_like(sb_scr)

    ri = lax.broadcasted_iota(i32, (c, c), 0)
    ci = lax.broadcasted_iota(i32, (c, c), 1)
    lane = lax.broadcasted_iota(i32, (c, LANE), 1)
    eye = (ri == ci).astype(f32)
    dirs = ((qf_ref, kf_ref, vf_ref, gf_ref, of_ref, sf_scr), (qb_ref, kb_ref, vb_ref, gb_ref, ob_ref, sb_scr))
    chains = []
    for direction, (q_ref, k_ref, v_ref, g_ref, o_ref, s_scr) in enumerate(dirs):
        incl = (ri >= ci) if direction == 0 else (ri <= ci)
        strict = (ri > ci) if direction == 0 else (ri < ci)
        tri = incl.astype(bf16)
        gbv = g_ref[...]
        hi, mid, lo = _split3(gbv)
        dec_all = _dot(tri, hi) + _dot(tri, mid) + _dot(tri, lo)
        tot_all = dec_all[c - 1:c, :] if direction == 0 else dec_all[0:1, :]
        for h in range(B_HEADS):
            lg = direction * B_HEADS + h
            lb = 2 * B_HEADS + lg
            chains.append(dict(h=h, q_ref=q_ref, k_ref=k_ref, v_ref=v_ref, o_ref=o_ref, s_scr=s_scr, incl=incl,
                               strict=strict, dec=dec_all[:, lg:lg + 1], beta=gbv[:, lb:lb + 1],
                               tot=tot_all[:, lg:lg + 1], sl=slice(h * LANE, (h + 1) * LANE)))
    for ch in chains:
        dh, dm, dl = (part.astype(f32) for part in _split3(ch["dec"]))
        lhs = jnp.where(lane == 0, dh, jnp.where(lane == 1, dm, jnp.where(lane == 2, dl,
                                                                          jnp.where(lane < 6, 1.0, 0.0))))
        rhs = jnp.where(lane < 3, 1.0, jnp.where(lane == 3, -dh, jnp.where(lane == 4, -dm,
                                                                           jnp.where(lane == 5, -dl, 0.0))))
        ch["diff"] = _dot_nt(lhs.astype(bf16), rhs.astype(bf16))
    for ch in chains:
        k_b = ch["k_ref"][:, ch["sl"]]
        q = ch["q_ref"][:, ch["sl"]].astype(f32)
        ch["kbeta"] = k_b.astype(f32) * ch["beta"]
        ch["kk"] = _dot_nt(jnp.concatenate([ch["kbeta"], q], axis=0).astype(bf16), k_b)
    for ch in chains:
        gamma = jnp.exp(jnp.where(ch["incl"], ch.pop("diff"), NEG))
        kk = ch.pop("kk")
        a_str = jnp.where(ch["strict"], kk[:c] * gamma, 0.0)
        ch["a_qk"] = (kk[c:] * gamma).astype(bf16)
        ch["x"] = eye - a_str
        a_b = a_str.astype(bf16)
        ch["pw"] = _dot(a_b, a_b)
    for _ in range(int(math.log2(c)) - 2):
        for ch in chains:
            xp = _dot(jnp.concatenate([ch["x"], ch["pw"]], axis=0).astype(bf16), ch["pw"].astype(bf16))
            ch["x"] = ch["x"] + xp[:c]
            ch["pw"] = xp[c:]
    for ch in chains:
        ch["x"] = ch["x"] + _dot(ch["x"].astype(bf16), ch.pop("pw").astype(bf16))
    for ch in chains:
        edec = jnp.exp(ch["dec"])
        v = ch["v_ref"][:, ch["sl"]].astype(f32)
        rhs = jnp.concatenate([v * ch["beta"], ch.pop("kbeta") * edec], axis=1).astype(bf16)
        ch["sol"] = _dot(ch.pop("x").astype(bf16), rhs)
    for ch in chains:
        sol = ch.pop("sol")
        q = ch["q_ref"][:, ch["sl"]].astype(f32)
        ch["state"] = ch["s_scr"][ch["h"]]
        ch["u"] = sol[:, :B_DV]
        lhs = jnp.concatenate([sol[:, B_DV:], q * jnp.exp(ch["dec"])], axis=0).astype(bf16)
        ch["ws"] = _dot(lhs, ch["state"].astype(bf16))
    for ch in chains:
        ws = ch.pop("ws")
        v_new = (ch.pop("u") - ws[:c]).astype(bf16)
        k = ch["k_ref"][:, ch["sl"]].astype(f32)
        kd = (k * jnp.exp(ch["tot"] - ch["dec"])).astype(bf16)
        ch["o_ref"][:, ch["sl"]] = ws[c:] + _dot(ch.pop("a_qk"), v_new)
        ch["s_scr"][ch["h"]] = ch.pop("state") * jnp.exp(ch["tot"]) + _dot_tn(kd, v_new)


def _gdn_scan(qkvn, gb, st):
    batch, nl, rows = st["batch"], st["nl"], st["rows"]
    c = GDN_CHUNK
    ncc, nlc = st["ctx_len"] // c, st["seq"] // c
    w = B_HEADS * LANE

    def fwd(b, s):
        return jnp.where(s < ncc, nl // c + b * ncc + s, b * nlc + (s - ncc))

    def bwd(b, s):
        return jnp.where(s < ncc, nl // c + b * ncc + (ncc - 1 - s), b * nlc + (nlc - 1 - (s - ncc)))

    def specs(rowf):
        return [pl.BlockSpec((c, w), lambda b, s, j=j: (rowf(b, s), j)) for j in range(3)] + \
               [pl.BlockSpec((c, LANE), lambda b, s: (rowf(b, s), 0))]

    return pl.pallas_call(
        _gdn_scan_kernel,
        grid=(batch, ncc + nlc),
        in_specs=specs(fwd) + specs(bwd),
        out_specs=[pl.BlockSpec((c, w), lambda b, s: (fwd(b, s), 0)), pl.BlockSpec((c, w), lambda b, s: (bwd(b, s), 0))],
        out_shape=[jax.ShapeDtypeStruct((rows, w), f32), jax.ShapeDtypeStruct((rows, w), f32)],
        scratch_shapes=[pltpu.VMEM((B_HEADS, B_DK, B_DV), f32), pltpu.VMEM((B_HEADS, B_DK, B_DV), f32)],
        compiler_params=_cparams(("arbitrary", "arbitrary")),
        name="gdn_scan",
    )(qkvn, qkvn, qkvn, gb, qkvn, qkvn, qkvn, gb)


def _gdn_out_kernel(of_ref, ob_ref, z_ref, gain_ref, y_ref):
    o = of_ref[...] + ob_ref[...]
    gain = gain_ref[...]
    for h in range(B_HEADS):
        sl = slice(h * LANE, (h + 1) * LANE)
        oh = o[:, sl]
        y = oh * lax.rsqrt(jnp.mean(oh * oh, axis=-1, keepdims=True) + NORM_EPS) * gain
        z = z_ref[:, sl].astype(f32)
        y_ref[:, sl] = (y * (z * jax.nn.sigmoid(z))).astype(bf16)


def _gdn_out(o_f, o_b, proj, gain, st):
    rows, tm = st["rows"], st["tm_big"]
    w = B_HEADS * LANE
    z_blk = (3 * st["d"] + REL_BZ) // w
    spec = pl.BlockSpec((tm, w), lambda i: (i, 0))
    return pl.pallas_call(
        _gdn_out_kernel,
        grid=(rows // tm,),
        in_specs=[spec, spec, pl.BlockSpec((tm, w), lambda i: (i, z_blk)), pl.BlockSpec((1, B_DV), lambda i: (0, 0))],
        out_specs=spec,
        out_shape=jax.ShapeDtypeStruct((rows, w), bf16),
        compiler_params=_cparams(("arbitrary",)),
        name="gdn_out",
    )(o_f, o_b, proj, gain.reshape(1, B_DV))


def _mixer_b(proj, pab, w_conv, a_log, dt_bias, out_gain, st):
    qkvn, gb = _gdn_prep(proj, pab, w_conv, a_log, dt_bias, st)
    o_f, o_b = _gdn_scan(qkvn, gb, st)
    return _gdn_out(o_f, o_b, proj, out_gain, st)


def _merge_kernel(ya_ref, yb_ref, yc_ref, wa_ref, wb_ref, wc_ref, ga_ref, gb_ref, gc_ref, o_ref):
    acc = None
    for y_ref, w_ref, g_ref in ((ya_ref, wa_ref, ga_ref), (yb_ref, wb_ref, gb_ref), (yc_ref, wc_ref, gc_ref)):
        t = jax.nn.sigmoid(g_ref[...].astype(f32)) * _dot(y_ref[...], w_ref[...])
        acc = t if acc is None else acc + t
    o_ref[...] = acc.astype(bf16)


def _merge(ya, yb, yc, w_branch_b, proj, st, rows):
    d, tm = st["d"], st["tm_big"]
    tn = _pick((1024, 512, 256, 128), d)
    bw = ya.shape[1]
    nj = d // tn
    y_spec = pl.BlockSpec((tm, bw), lambda i, j: (i, 0))
    w_specs = [pl.BlockSpec((bw, tn), lambda i, j, r=r: (r, j)) for r in range(3)]
    g_specs = [pl.BlockSpec((tm, tn), lambda i, j, r=r: (i, r * nj + j)) for r in range(3)]
    return pl.pallas_call(
        _merge_kernel,
        grid=(rows // tm, nj),
        in_specs=[y_spec, y_spec, y_spec] + w_specs + g_specs,
        out_specs=pl.BlockSpec((tm, tn), lambda i, j: (i, j)),
        out_shape=jax.ShapeDtypeStruct((rows, d), bf16),
        compiler_params=_cparams(("arbitrary", "arbitrary")),
        name="branch_merge",
    )(ya, yb, yc, w_branch_b, w_branch_b, w_branch_b, proj, proj, proj)


def _out_proj_kernel(m_ref, w_ref, x_ref, g_ref, o_ref):
    o_ref[...] = x_ref[...] + g_ref[0] * _dot(m_ref[...], w_ref[...])


def _out_proj(mixed, w_o_b, xall, gate, st, rows):
    d, tm = st["d"], st["tm_big"]
    tn = _pick((1024, 512, 256, 128), d)
    group = _group_of_tile(st, tm)
    return pl.pallas_call(
        _out_proj_kernel,
        grid=(rows // tm, d // tn),
        in_specs=[pl.BlockSpec((tm, d), lambda i, j: (i, 0)),
                  pl.BlockSpec((d, tn), lambda i, j: (0, j)),
                  pl.BlockSpec((tm, tn), lambda i, j: (i, j)),
                  pl.BlockSpec((1, 1, tn), lambda i, j: (group(i), 0, j))],
        out_specs=pl.BlockSpec((tm, tn), lambda i, j: (i, j)),
        out_shape=jax.ShapeDtypeStruct((rows, d), f32),
        compiler_params=_cparams(("arbitrary", "arbitrary")),
        name="out_proj",
    )(mixed, w_o_b, xall, gate)


def _norm_router_kernel(x_ref, s_ref, sh_ref, wr_ref, br_ref, h_ref, aff_ref):
    x = x_ref[...]
    h = x * lax.rsqrt(jnp.mean(x * x, axis=-1, keepdims=True) + NORM_EPS) * s_ref[0] + sh_ref[0]
    h_ref[...] = h
    logits = _dot_nt(wr_ref[...], h.astype(bf16)) + br_ref[...]
    e = jnp.exp(logits - jnp.max(logits, axis=0, keepdims=True))
    aff_ref[...] = e / jnp.sum(e, axis=0, keepdims=True)


def _norm_router(x, lay, w_router, b_router, st, rows):
    d, tm = st["d"], st["tm_big"]
    group = _group_of_tile(st, tm)
    ne = w_router.shape[1]
    return pl.pallas_call(
        _norm_router_kernel,
        grid=(rows // tm,),
        in_specs=[pl.BlockSpec((tm, d), lambda i: (i, 0)),
                  pl.BlockSpec((1, 1, d), lambda i: (group(i), 0, 0)),
                  pl.BlockSpec((1, 1, d), lambda i: (group(i), 0, 0)),
                  pl.BlockSpec((ne, d), lambda i: (0, 0)),
                  pl.BlockSpec((ne, 1), lambda i: (0, 0))],
        out_specs=[pl.BlockSpec((tm, d), lambda i: (i, 0)), pl.BlockSpec((ne, tm), lambda i: (0, i))],
        out_shape=[jax.ShapeDtypeStruct((rows, d), f32), jax.ShapeDtypeStruct((ne, rows), f32)],
        compiler_params=_cparams(("arbitrary",)),
        name="norm_router",
    )(x, lay["scale2"], lay["shift2"], w_router.T.astype(bf16), b_router.reshape(ne, 1).astype(f32))


PREFIX_W = 256


def _prefix_excl(mask):
    ne, t = mask.shape
    w = min(PREFIX_W, t)
    r = lax.broadcasted_iota(i32, (w, w), 0)
    c = lax.broadcasted_iota(i32, (w, w), 1)
    upper = jnp.where(r < c, 1.0, 0.0).astype(bf16)
    off = jnp.zeros((ne, 1), f32)
    outs = []
    for j in range(t // w):
        xb = jnp.where(mask[:, j * w:(j + 1) * w], 1.0, 0.0)
        outs.append(_dot(xb.astype(bf16), upper) + off)
        off = off + jnp.sum(xb, axis=1, keepdims=True)
    return jnp.concatenate(outs, axis=1) if len(outs) > 1 else outs[0]


def _topk_kernel(aff_ref, idx_ref, gate_ref, *, t, cap, base0):
    g = pl.program_id(0)
    aff = aff_ref[...]
    ne = aff.shape[0]
    bits = pltpu.bitcast(aff, i32)
    thr = jnp.zeros((ne, 1), i32)
    for bit in range(30, -1, -1):
        cand = thr | (1 << bit)
        cnt = jnp.sum(jnp.where(bits >= cand, 1.0, 0.0), axis=1, keepdims=True)
        thr = jnp.where(cnt >= cap, cand, thr)
    gt = bits > thr
    eq = bits == thr
    need = cap - jnp.sum(jnp.where(gt, 1.0, 0.0), axis=1, keepdims=True)
    sel = gt | (eq & (_prefix_excl(eq) < need))
    pos = _prefix_excl(sel).astype(i32)
    tok = lax.broadcasted_iota(i32, (1, t), 1)
    t_hi = (tok >> 8).astype(f32)
    t_lo = (tok & 255).astype(f32)
    slot = lax.broadcasted_iota(i32, (cap, t), 0)
    base = base0 + g * t
    for e in range(ne):
        onehot = jnp.where((slot == pos[e:e + 1, :]) & sel[e:e + 1, :], 1.0, 0.0).astype(bf16)
        g_hi, g_mid, g_lo = _split3(aff[e:e + 1, :])
        vals = jnp.concatenate([t_hi.astype(bf16), t_lo.astype(bf16), g_hi, g_mid, g_lo,
                                jnp.zeros((3, t), bf16)], axis=0)
        out = _dot_nt(vals, onehot)
        idx_ref[0, e:e + 1, :] = base + (out[0:1] * 256.0 + out[1:2]).astype(i32)
        gate_ref[0, e:e + 1, :] = out[2:3] + out[3:4] + out[4:5]


def _topk(aff_t, n_groups, t, cap, col_blk0, base0):
    ne = aff_t.shape[0]
    return pl.pallas_call(
        functools.partial(_topk_kernel, t=t, cap=cap, base0=base0),
        grid=(n_groups,),
        in_specs=[pl.BlockSpec((ne, t), lambda g: (0, col_blk0 + g))],
        out_specs=[pl.BlockSpec((1, ne, cap), lambda g: (g, 0, 0)), pl.BlockSpec((1, ne, cap), lambda g: (g, 0, 0))],
        out_shape=[jax.ShapeDtypeStruct((n_groups, ne, cap), i32), jax.ShapeDtypeStruct((n_groups, ne, cap), f32)],
        compiler_params=_cparams(("arbitrary",)),
        name="expert_topk",
    )(aff_t)


BF16_ROWS = 16


def _pieces(m, want):
    for n in range(want, 0, -1):
        if m % (n * BF16_ROWS) == 0:
            return n
    raise ValueError(m)


def _ffn1_kernel(idx_ref, h_hbm, wg_ref, wu_ref, o_ref, stage, xs, sem, *, m):
    chunk = m // _pieces(m, 2)

    @pl.when(pl.program_id(1) == 0)
    def _():
        for c0 in range(0, m, chunk):
            def start(r, carry, c0=c0):
                row = idx_ref[0, 0, c0 + r]
                pltpu.make_async_copy(h_hbm.at[pl.ds(row, 1), :], stage.at[pl.ds(r, 1), :], sem).start()
                return carry

            def wait(r, carry):
                pltpu.make_async_copy(h_hbm.at[pl.ds(0, 1), :], stage.at[pl.ds(r, 1), :], sem).wait()
                return carry

            lax.fori_loop(0, chunk, start, 0, unroll=8)
            lax.fori_loop(0, chunk, wait, 0, unroll=8)
            xs[c0:c0 + chunk, :] = stage[...].astype(bf16)

    wg = wg_ref[0].astype(bf16)
    wu = wu_ref[0].astype(bf16)
    rc = m // _pieces(m, 4)
    for r0 in range(0, m, rc):
        xr = xs[r0:r0 + rc, :]
        a = _dot(xr, wg)
        o_ref[0, r0:r0 + rc, :] = (a * jax.nn.sigmoid(a) * _dot(xr, wu)).astype(bf16)


def _ffn1(idx, h2, w_gate, w_up, layer):
    ne, m = idx.shape
    d, ff = w_gate.shape[2], w_gate.shape[3]
    tf = _pick((512, 256, 128), ff)
    w_gate = w_gate.reshape(-1, d, ff)
    w_up = w_up.reshape(-1, d, ff)
    return pl.pallas_call(
        functools.partial(_ffn1_kernel, m=m),
        grid=(ne, ff // tf),
        in_specs=[pl.BlockSpec((1, 1, m), lambda e, f: (e, 0, 0), memory_space=pltpu.SMEM),
                  pl.BlockSpec(memory_space=pl.ANY),
                  pl.BlockSpec((1, d, tf), lambda e, f: (layer * ne + e, 0, f)),
                  pl.BlockSpec((1, d, tf), lambda e, f: (layer * ne + e, 0, f))],
        out_specs=pl.BlockSpec((1, m, tf), lambda e, f: (e, 0, f)),
        out_shape=jax.ShapeDtypeStruct((ne, m, ff), bf16),
        scratch_shapes=[pltpu.VMEM((m // _pieces(m, 2), d), f32), pltpu.VMEM((m, d), bf16),
                        pltpu.SemaphoreType.DMA(())],
        compiler_params=_cparams(("arbitrary", "arbitrary")),
        name="expert_ffn_in",
    )(idx.reshape(ne, 1, m), h2, w_gate, w_up)


def _ffn2_kernel(h_ref, w_ref, g_ref, o_ref):
    o_ref[0] = _dot(h_ref[0], w_ref[0].astype(bf16)) * g_ref[0]


def _ffn2(hid, w_down, gates, layer):
    ne, m, ff = hid.shape
    d = w_down.shape[3]
    td = _pick((512, 256, 128), d)
    w_down = w_down.reshape(-1, ff, d)
    return pl.pallas_call(
        _ffn2_kernel,
        grid=(ne, d // td),
        in_specs=[pl.BlockSpec((1, m, ff), lambda e, j: (e, 0, 0)),
                  pl.BlockSpec((1, ff, td), lambda e, j: (layer * ne + e, 0, j)),
                  pl.BlockSpec((1, m, 1), lambda e, j: (e, 0, 0))],
        out_specs=pl.BlockSpec((1, m, td), lambda e, j: (e, 0, j)),
        out_shape=jax.ShapeDtypeStruct((ne, m, d), f32),
        compiler_params=_cparams(("arbitrary", "arbitrary")),
        name="expert_ffn_out",
    )(hid, w_down, gates.reshape(ne, m, 1))


def _combine_kernel(*refs, cap, aliased):
    idx_ref, y_ref = refs[0], refs[1]
    o_ref = refs[-1]

    @pl.when(pl.program_id(2) == 0)
    def _():
        o_ref[...] = jnp.zeros_like(o_ref)

    def body(c, carry):
        r = idx_ref[0, 0, c]
        o_ref[pl.ds(r, 1), :] = o_ref[pl.ds(r, 1), :] + y_ref[0, pl.ds(c, 1), :]
        return carry

    lax.fori_loop(0, cap, body, 0, unroll=8)


def _combine(local_idx, y, n_groups, t, cap, y_blk0, out_blk0, out_rows, prev=None):
    ne, _, d = y.shape
    td = _pick((1024, 512, 256, 128), d)
    in_specs = [pl.BlockSpec((1, 1, cap), lambda g, j, e: (e * n_groups + g, 0, 0), memory_space=pltpu.SMEM),
                pl.BlockSpec((1, cap, td), lambda g, j, e: (e, y_blk0 + g, j))]
    args = [local_idx.reshape(ne * n_groups, 1, cap), y]
    aliases = {}
    if prev is not None:
        in_specs.append(pl.BlockSpec(memory_space=pl.ANY))
        args.append(prev)
        aliases = {2: 0}
    return pl.pallas_call(
        functools.partial(_combine_kernel, cap=cap, aliased=prev is not None),
        grid=(n_groups, d // td, ne),
        in_specs=in_specs,
        out_specs=pl.BlockSpec((t, td), lambda g, j, e: (out_blk0 + g, j)),
        out_shape=jax.ShapeDtypeStruct((out_rows, d), f32),
        input_output_aliases=aliases,
        compiler_params=_cparams(("arbitrary", "arbitrary", "arbitrary")),
        name="expert_combine",
    )(*args)


def _moe(x_mid, lay, w_router, b_router, w_gate, w_up, w_down, layer, st, with_ctx):
    batch, seq, ctx_len, nl = st["batch"], st["seq"], st["ctx_len"], st["nl"]
    ne = w_router.shape[1]
    rows = nl + (st["nc"] if with_ctx else 0)
    h2, aff_t = _norm_router(x_mid, lay, w_router, b_router, st, rows)
    cap = EC_CAPACITY * seq // ne
    idx_l, gate_l = _topk(aff_t, batch, seq, cap, 0, 0)
    idx = jnp.swapaxes(idx_l, 0, 1).reshape(ne, batch * cap)
    gates = jnp.swapaxes(gate_l, 0, 1).reshape(ne, batch * cap)
    loc_l = jnp.swapaxes(idx_l - (jnp.arange(batch, dtype=i32) * seq)[:, None, None], 0, 1)
    if with_ctx:
        cap_c = EC_CAPACITY * ctx_len // ne
        idx_c, gate_c = _topk(aff_t, batch, ctx_len, cap_c, nl // ctx_len, nl)
        idx = jnp.concatenate([idx, jnp.swapaxes(idx_c, 0, 1).reshape(ne, batch * cap_c)], axis=1)
        gates = jnp.concatenate([gates, jnp.swapaxes(gate_c, 0, 1).reshape(ne, batch * cap_c)], axis=1)
        loc_c = jnp.swapaxes(idx_c - (nl + jnp.arange(batch, dtype=i32) * ctx_len)[:, None, None], 0, 1)
    hid = _ffn1(idx, h2, w_gate, w_up, layer)
    y = _ffn2(hid, w_down, gates, layer)
    out = _combine(loc_l, y, batch, seq, cap, 0, 0, rows)
    if with_ctx:
        out = _combine(loc_c, y, batch, ctx_len, cap_c, batch * cap // cap_c, nl // ctx_len, rows, prev=out)
    return out


def _pack_w_in(w):
    d = w.shape[0]
    sizes = (A_HEADS * HEAD_DIM, A_KV_HEADS * HEAD_DIM, A_KV_HEADS * HEAD_DIM,
             B_HEADS * (2 * B_DK + B_DV), B_HEADS * B_DV, 2 * B_HEADS, 2 * B_HEADS,
             C_Q_RANK, C_KV_RANK, C_ROPE, 3 * d)
    aq, ak, av, bqkv, bz, ba, bb, cq, ckv, ckr, gt = jnp.split(w, np.cumsum(sizes)[:-1].tolist(), axis=1)
    z = lambda n: jnp.zeros((d, n), w.dtype)
    packed = jnp.concatenate(
        [gt, cq, ckv, ckr, z(LANE - C_ROPE), ba, bb, z(LANE - 4 * B_HEADS), aq, ak, av, bqkv, bz], axis=1)
    assert packed.shape[1] == 3 * d + REL_END
    return packed.astype(bf16)


def _pack_cq_up(w):
    r = w.shape[0]
    w3 = w.reshape(r, C_HEADS, C_NOPE + C_ROPE)
    w3 = jnp.concatenate([w3, jnp.zeros((r, C_HEADS, 2 * LANE - C_NOPE - C_ROPE), w.dtype)], axis=2)
    return w3.reshape(r, C_HEADS * 2 * LANE).astype(bf16)


def _pack_ckv_up(w):
    r = w.shape[0]
    w3 = w.reshape(r, C_HEADS, C_NOPE + C_V)
    return jnp.concatenate([w3[:, :, :C_NOPE].reshape(r, -1), w3[:, :, C_NOPE:].reshape(r, -1)], axis=1).astype(bf16)


def _prepare_statics(batch, seq, ctx_len, d):
    nl, nc = batch * seq, batch * ctx_len
    assert seq % GRID_W == 0 and seq % WBLK == 0 and ctx_len % WBLK == 0 and nl % ctx_len == 0
    st = dict(batch=batch, seq=seq, ctx_len=ctx_len, d=d, nl=nl, nc=nc, rows=nl + nc)
    st["tm_big"] = _pick((1024, 512, 256, 128), seq, nc)
    st["tm_proj"] = _pick((512, 256, 128), seq, nc)
    st["tm_seq"] = _pick((256, 128, 64), seq, ctx_len)
    st["tq"] = _pick((256, 128), seq, ctx_len)
    st["tabs_a"] = _rope_tables(batch, seq, nc, HEAD_DIM)
    st["tabs_c"] = _rope_tables(batch, seq, nc, C_ROPE)
    return st


def _group_of_tile(st, tm):
    per_seq = st["seq"] // tm
    batch = st["batch"]
    return lambda i: jnp.minimum(i // per_seq, batch)


def _layer_mods(mod_l, batch, g1, g2):
    sh1, sc1, gt1, sh2, sc2, gt2 = jnp.split(mod_l[:batch + 1], 6, axis=-1)
    r3 = lambda a: a[:, None, :]
    return dict(scale1=r3(g1[None, :] * (1.0 + sc1)), shift1=r3(sh1), gate1=r3(gt1),
                scale2=r3(g2[None, :] * (1.0 + sc2)), shift2=r3(sh2), gate2=r3(gt2))


def _residual_kernel(x_ref, d_ref, g_ref, o_ref):
    o_ref[...] = x_ref[...] + g_ref[0] * d_ref[...]


def _residual(x, delta, gate, st):
    rows, d = x.shape
    tm = st["tm_proj"]
    group = _group_of_tile(st, tm)
    spec = pl.BlockSpec((tm, d), lambda i: (i, 0))
    return pl.pallas_call(
        _residual_kernel,
        grid=(rows // tm,),
        in_specs=[spec, spec, pl.BlockSpec((1, 1, d), lambda i: (group(i), 0, 0))],
        out_specs=spec,
        out_shape=jax.ShapeDtypeStruct((rows, d), f32),
        compiler_params=_cparams(("arbitrary",)),
        name="gated_residual",
    )(x, delta, gate)


def _proj_in(xall, lay, w_in_p, st):
    d = st["d"]
    tm = st["tm_big"]
    tn = _pick((1280, 768, 640, 384, 256, 128), w_in_p.shape[1])
    return _norm_matmul(xall, 0, d, lay["scale1"], lay["shift1"], w_in_p, _group_of_tile(st, tm), tm, tn,
                        side_col=3 * d + REL_AB, name="proj_in")


def _mixer_a(proj, sink, st, with_ctx):
    d3 = 3 * st["d"]
    aqk = _prep_a(proj, st["tabs_a"], (d3 + REL_AQ) // ((A_HEADS + A_KV_HEADS) * LANE), st["tm_big"])
    return _win_attn(aqk, proj, (d3 + REL_AV) // (A_KV_HEADS * LANE), sink, st["batch"], st["seq"], st["ctx_len"],
                     with_ctx)


def _mixer_c(proj, q_gain, cq_up_p, kv_gain, ckv_up_p, st, with_ctx):
    d3 = 3 * st["d"]
    tm = st["tm_big"]
    one = lambda i: 0
    qp = _norm_matmul(proj, (d3 + REL_CQ) // C_Q_RANK, C_Q_RANK, q_gain.reshape(1, 1, -1),
                      jnp.zeros((1, 1, C_Q_RANK), f32), cq_up_p, one, tm, cq_up_p.shape[1], name="mla_q_up")[0]
    kvp = _norm_matmul(proj, (d3 + REL_CKV) // C_KV_RANK, C_KV_RANK, kv_gain.reshape(1, 1, -1),
                       jnp.zeros((1, 1, C_KV_RANK), f32), ckv_up_p, one, tm, ckv_up_p.shape[1], name="mla_kv_up")[0]
    cq, ck = _prep_c(qp, kvp, proj, st["tabs_c"], (d3 + REL_CKR) // LANE, tm)
    return _mla_attn(cq, ck, kvp, st["batch"], st["seq"], st["ctx_len"], st["tq"], with_ctx)


def kernel(x, c, ctx, c_ctx, w_mod, b_mod, g_norm1, g_norm2, w_in, a_sink, b_conv, b_a_log, b_dt_bias, b_out_gain,
           c_q_gain, c_q_up, c_kv_gain, c_kv_up, w_branch, w_o, w_router, b_router, w_exp_gate, w_exp_up,
           w_exp_down, g_final):
    batch, seq, d = x.shape
    ctx_len = ctx.shape[1]
    depth = w_in.shape[0]
    assert batch + 1 <= 8
    st = _prepare_statics(batch, seq, ctx_len, d)
    nl, rows_all = st["nl"], st["rows"]
    c_all = jnp.concatenate([c, c_ctx[None], jnp.zeros((8 - batch - 1, d), f32)], axis=0)
    mods = _mod(c_all, w_mod, b_mod)
    xall = jnp.concatenate([x.reshape(nl, d), ctx.reshape(batch * ctx_len, d)], axis=0)
    delta = gate_prev = None
    for l in range(depth):
        need_ctx = l < depth - 1
        lay = _layer_mods(mods[l], batch, g_norm1[l], g_norm2[l])
        w_in_p = _pack_w_in(w_in[l])
        if delta is not None:
            xall = _residual(xall, delta, gate_prev, st)
        proj, pab = _proj_in(xall, lay, w_in_p, st)
        ya = _mixer_a(proj, a_sink[l], st, need_ctx)
        yb = _mixer_b(proj, pab, b_conv[l], b_a_log[l], b_dt_bias[l], b_out_gain[l], st)
        yc = _mixer_c(proj, c_q_gain[l], _pack_cq_up(c_q_up[l]), c_kv_gain[l], _pack_ckv_up(c_kv_up[l]), st, need_ctx)
        rows = rows_all if need_ctx else nl
        mixed = _merge(ya, yb, yc, w_branch[l].astype(bf16), proj, st, rows)
        xall = _out_proj(mixed, w_o[l].astype(bf16), xall, lay["gate1"], st, rows)
        delta = _moe(xall, lay, w_router[l], b_router[l], w_exp_gate, w_exp_up, w_exp_down, l, st, need_ctx)
        gate_prev = lay["gate2"]
    tm = st["tq"]
    out = _final_norm(xall, delta, gate_prev, g_final, _group_of_tile(st, tm), tm, nl)
    return out.reshape(batch, seq, d)
```
